```python
import jax
import jax.numpy as jnp
from jax import lax
import numpy as np

D_MODEL = 1024
BATCH = 8
SEQ = 2048
DEPTH = 2
DEC_BATCH = 128
DEC_SEQ = 8
PAST_LEN = 2048
PAGE_SIZE = 128

N_META = 16
QBLK = 128
CHUNK = 64
ROPE_THETA = 10000.0
EPS = 1e-6
NEG = -1e30
FOX_HEADS = 4
FOX_HD = 64
FOX_W = FOX_HEADS * FOX_HD
GLA_HEADS = 4
GLA_DK = 32
GLA_DV = 64
GLA_KW = GLA_HEADS * GLA_DK
GLA_VW = GLA_HEADS * GLA_DV
GLA_RANK = 16
GLA_TAU = 16.0
DSA_HEADS = 4
DSA_HD = 64
DSA_W = DSA_HEADS * DSA_HD
IDX_HEADS = 4
IDX_DIM = 64
DSA_TOPK = 256
SSD_HEADS = 4
SSD_HD = 64
SSD_W = SSD_HEADS * SSD_HD
SSD_GROUPS = 2
SSD_STATE = 64
SSD_BC = SSD_GROUPS * SSD_STATE
SSD_XBC = SSD_W + 2 * SSD_BC
SSD_CONV = 4
N_BRANCH = 4
D_FF = 4 * D_MODEL
IN_SPLITS = (('fox_q', FOX_W), ('fox_k', FOX_W), ('fox_v', FOX_W), ('fox_f', FOX_HEADS),
             ('gla_q', GLA_KW), ('gla_k', GLA_KW), ('gla_v', GLA_VW), ('gla_a', GLA_RANK), ('gla_r', GLA_VW),
             ('dsa_q', DSA_W), ('dsa_k', DSA_W), ('dsa_v', DSA_W),
             ('idx_q', IDX_HEADS * IDX_DIM), ('idx_w', IDX_HEADS), ('idx_k', IDX_DIM),
             ('ssd_z', SSD_W), ('ssd_xbc', SSD_XBC), ('ssd_dt', SSD_HEADS),
             ('gates', N_BRANCH * D_MODEL))
IN_COLS = sum(w for _, w in IN_SPLITS)

kernel_name = 'hybrid_fox_gla_dsa_ssd_step'


def rmsnorm(x, g):
    xf = x.astype(jnp.float32)
    y = xf * lax.rsqrt(jnp.mean(jnp.square(xf), axis=-1, keepdims=True) + EPS)
    return (y * g.astype(jnp.float32)).astype(x.dtype)


def rope(x, pos):
    half = x.shape[-1] // 2
    freqs = ROPE_THETA ** (-jnp.arange(half, dtype=jnp.float32) / half)
    ang = pos.astype(jnp.float32)[:, None] * freqs[None, :]
    cos = jnp.cos(ang)[None, :, None, :]
    sin = jnp.sin(ang)[None, :, None, :]
    xf = x.astype(jnp.float32)
    x1, x2 = xf[..., :half], xf[..., half:]
    return jnp.concatenate([x1 * cos - x2 * sin, x2 * cos + x1 * sin], axis=-1).astype(x.dtype)


def split_cols(p):
    out = {}
    off = 0
    for name, w in IN_SPLITS:
        out[name] = p[..., off:off + w]
        off += w
    return out


def query_blocks(fn, arrays, lead):
    outs = []
    if lead:
        outs.append(fn(*[a[:, :lead] for a in arrays]))
        arrays = [a[:, lead:] for a in arrays]
    n = arrays[0].shape[1]
    if n >= QBLK and n % QBLK == 0:
        nb = n // QBLK
        blk = [jnp.moveaxis(a.reshape(a.shape[0], nb, QBLK, *a.shape[2:]), 1, 0) for a in arrays]
        y = jnp.moveaxis(lax.map(lambda xs: fn(*xs), blk), 0, 1)
        outs.append(y.reshape(y.shape[0], n, *y.shape[3:]))
    else:
        outs.append(fn(*arrays))
    return jnp.concatenate(outs, axis=1) if len(outs) > 1 else outs[0]


def chunk_scan(step, xs, state, c):
    n = xs[0].shape[1]
    nc = n // c
    xs_c = [jnp.moveaxis(a.reshape(a.shape[0], nc, c, *a.shape[2:]), 1, 0) for a in xs]
    state, ys = lax.scan(step, state, xs_c)
    ys = jnp.moveaxis(ys, 0, 1)
    return state, ys.reshape(ys.shape[0], n, *ys.shape[3:])


def run_chunks(scan_fn, xs, state, lead):
    ys = []
    if lead:
        state, y = scan_fn([a[:, :lead] for a in xs], state, lead)
        ys.append(y)
        xs = [a[:, lead:] for a in xs]
    n = xs[0].shape[1]
    c = CHUNK if n % CHUNK == 0 else n
    state, y = scan_fn(xs, state, c)
    ys.append(y)
    return state, (jnp.concatenate(ys, axis=1) if len(ys) > 1 else ys[0])


def gla_scan(xs, S, c):
    def step(S, inp):
        qc, kc, vc, gc = inp
        b = jnp.cumsum(gc, axis=1)
        t = jnp.arange(qc.shape[1])
        causal = (t[:, None] >= t[None, :])[None, :, :, None, None]
        decay = jnp.exp(jnp.where(causal, b[:, :, None] - b[:, None, :], NEG))
        att = jnp.einsum('bthd,bshd,btshd->bhts', qc, kc, decay)
        y = jnp.einsum('bthd,bhde->bthe', qc * jnp.exp(b), S) + jnp.einsum('bhts,bshe->bthe', att, vc)
        bl = b[:, -1]
        S = jnp.exp(bl)[..., None] * S + jnp.einsum('bshd,bshe->bhde', kc * jnp.exp(bl[:, None] - b), vc)
        return S, y
    return chunk_scan(step, xs, S, c)


def ssd_scan(xs, h, c):
    def step(h, inp):
        xc, bc, cc, dtc, ac = inp
        cum = jnp.cumsum(ac, axis=1)
        t = jnp.arange(xc.shape[1])
        causal = (t[:, None] >= t[None, :])[None, :, :, None]
        seg = jnp.exp(jnp.where(causal, cum[:, :, None] - cum[:, None, :], NEG))
        m = jnp.einsum('bthn,bshn,btsh->bhts', cc, bc, seg)
        y = (jnp.einsum('bthn,bhpn->bthp', cc, h) * jnp.exp(cum)[..., None]
             + jnp.einsum('bhts,bsh,bshp->bthp', m, dtc, xc))
        last = cum[:, -1]
        w = jnp.exp(last[:, None] - cum) * dtc
        h = jnp.exp(last)[..., None, None] * h + jnp.einsum('bsh,bshn,bshp->bhpn', w, bc, xc)
        return h, y
    return chunk_scan(step, xs, h, c)


def fox_attend(q, k_all, v_all, cq, ck, qpos, lead):
    kpos = jnp.arange(k_all.shape[1])
    ck_t = jnp.moveaxis(ck, 2, 1)[:, :, None, :]
    def blk(qb, cqb, pb):
        s = jnp.einsum('bqhd,bkhd->bhqk', qb, k_all).astype(jnp.float32) * FOX_HD ** -0.5
        s = s + jnp.moveaxis(cqb, 2, 1)[..., None] - ck_t
        mask = kpos[None, None, None, :] <= pb[:, None, :, None]
        p = jax.nn.softmax(jnp.where(mask, s, NEG), axis=-1)
        return jnp.einsum('bhqk,bkhd->bqhd', p.astype(v_all.dtype), v_all)
    return query_blocks(blk, (q, cq, qpos), lead)


def dsa_attend(q, qi, wi, k_all, v_all, ki_all, qpos, lead, topk):
    kpos = jnp.arange(k_all.shape[1])
    def blk(qb, qib, wib, pb):
        sc = jax.nn.relu(jnp.einsum('bqhd,bkd->bqhk', qib, ki_all).astype(jnp.float32))
        sc = jnp.einsum('bqhk,bqh->bqk', sc, wib.astype(jnp.float32))
        sc = jnp.where(kpos[None, None, :] <= pb[:, :, None], sc, NEG)
        _, sel = lax.top_k(sc, topk)
        valid = sel <= pb[:, :, None]
        ks = jax.vmap(lambda kb, sb: kb[sb])(k_all, sel)
        vs = jax.vmap(lambda vb, sb: vb[sb])(v_all, sel)
        s = jnp.einsum('bqhd,bqkhd->bhqk', qb, ks).astype(jnp.float32) * DSA_HD ** -0.5
        p = jax.nn.softmax(jnp.where(valid[:, None], s, NEG), axis=-1)
        return jnp.einsum('bhqk,bqkhd->bqhd', p.astype(vs.dtype), vs)
    return query_blocks(blk, (q, qi, wi, qpos), lead)


def causal_conv(x, buf, w, b):
    L = x.shape[1]
    full = jnp.concatenate([buf.astype(x.dtype), x], axis=1)
    y = b
    for i in range(SSD_CONV):
        y = y + full[:, i:i + L] * w[i]
    return jax.nn.silu(y), full[:, L:]


def gather_pages(pool, table):
    rows = pool[table]
    return rows.reshape(table.shape[0], table.shape[1] * pool.shape[1], *pool.shape[2:])


def mix_layer(u, pos, lead, lp, past):
    bsz, L = u.shape[0], u.shape[1]
    dt_ = u.dtype
    f32 = jnp.float32
    pr = split_cols(jnp.einsum('bld,de->ble', u, lp['w_in']) + lp['b_in'])
    qpos = pos[None, :]
    fq = pr['fox_q'].reshape(bsz, L, FOX_HEADS, FOX_HD)
    fk = pr['fox_k'].reshape(bsz, L, FOX_HEADS, FOX_HD)
    fv = pr['fox_v'].reshape(bsz, L, FOX_HEADS, FOX_HD)
    flogf = jax.nn.log_sigmoid(pr['fox_f'].astype(f32))
    dq = rope(pr['dsa_q'].reshape(bsz, L, DSA_HEADS, DSA_HD), pos)
    dk = rope(pr['dsa_k'].reshape(bsz, L, DSA_HEADS, DSA_HD), pos)
    dv = pr['dsa_v'].reshape(bsz, L, DSA_HEADS, DSA_HD)
    iq = rope(pr['idx_q'].reshape(bsz, L, IDX_HEADS, IDX_DIM), pos)
    ik = rope(pr['idx_k'][:, :, None, :], pos)[:, :, 0, :]
    if past is None:
        fk_all, fv_all, flogf_all = fk, fv, flogf
        dk_all, dv_all, ik_all = dk, dv, ik
        gla0 = jnp.zeros((bsz, GLA_HEADS, GLA_DK, GLA_DV), f32)
        ssd0 = jnp.zeros((bsz, SSD_HEADS, SSD_HD, SSD_STATE), f32)
        conv0 = jnp.zeros((bsz, SSD_CONV - 1, SSD_XBC), dt_)
    else:
        fk_all = jnp.concatenate([past['fox_k'].astype(dt_), fk], axis=1)
        fv_all = jnp.concatenate([past['fox_v'].astype(dt_), fv], axis=1)
        flogf_all = jnp.concatenate([past['fox_logf'].astype(f32), flogf], axis=1)
        dk_all = jnp.concatenate([past['dsa_k'].astype(dt_), dk], axis=1)
        dv_all = jnp.concatenate([past['dsa_v'].astype(dt_), dv], axis=1)
        ik_all = jnp.concatenate([past['dsa_kidx'].astype(dt_), ik], axis=1)
        gla0 = past['gla'].astype(f32)
        ssd0 = past['ssd'].astype(f32)
        conv0 = past['ssd_conv']
    c_all = jnp.cumsum(flogf_all, axis=1)
    n_past = c_all.shape[1] - L
    y_fox = fox_attend(fq, fk_all, fv_all, c_all[:, n_past:], c_all, qpos, lead)
    topk = min(DSA_TOPK, dk_all.shape[1] // 4)
    y_dsa = dsa_attend(dq, iq, pr['idx_w'], dk_all, dv_all, ik_all, qpos, lead, topk)
    gq = pr['gla_q'].reshape(bsz, L, GLA_HEADS, GLA_DK) * GLA_DK ** -0.5
    gk = pr['gla_k'].reshape(bsz, L, GLA_HEADS, GLA_DK)
    gv = pr['gla_v'].reshape(bsz, L, GLA_HEADS, GLA_DV)
    ga_logit = jnp.einsum('blr,re->ble', pr['gla_a'], lp['gla_w_a2']) + lp['gla_b_a2']
    ga = (jax.nn.log_sigmoid(ga_logit.astype(f32)) / GLA_TAU).reshape(bsz, L, GLA_HEADS, GLA_DK)
    gla_s, go = run_chunks(gla_scan, [gq, gk, gv, ga], gla0, lead)
    y_gla = rmsnorm(go, lp['gla_norm']).reshape(bsz, L, GLA_VW) * jax.nn.silu(pr['gla_r'])
    xbc, conv_new = causal_conv(pr['ssd_xbc'], conv0, lp['ssd_conv_w'], lp['ssd_conv_b'])
    rep = SSD_HEADS // SSD_GROUPS
    sx = xbc[..., :SSD_W].reshape(bsz, L, SSD_HEADS, SSD_HD)
    sb = jnp.repeat(xbc[..., SSD_W:SSD_W + SSD_BC].reshape(bsz, L, SSD_GROUPS, SSD_STATE), rep, axis=2)
    sc = jnp.repeat(xbc[..., SSD_W + SSD_BC:].reshape(bsz, L, SSD_GROUPS, SSD_STATE), rep, axis=2)
    sdt = jax.nn.softplus(pr['ssd_dt'].astype(f32) + lp['ssd_dt_bias'].astype(f32))
    a_neg = -jnp.exp(lp['ssd_a_log'].astype(f32))
    ssd_s, sy = run_chunks(ssd_scan, [sx, sb, sc, sdt, sdt * a_neg], ssd0, lead)
    sy = sy + lp['ssd_d'][:, None] * sx
    y_ssd = rmsnorm(sy.reshape(bsz, L, SSD_W) * jax.nn.silu(pr['ssd_z']), lp['ssd_norm'])
    gates = jax.nn.sigmoid(pr['gates']).reshape(bsz, L, N_BRANCH, D_MODEL)
    br = lambda y, w: jnp.einsum('blc,cd->bld', y.astype(dt_), w)
    merged = (gates[:, :, 0] * br(y_fox.reshape(bsz, L, FOX_W), lp['w_br_fox'])
              + gates[:, :, 1] * br(y_gla, lp['w_br_gla'])
              + gates[:, :, 2] * br(y_dsa.reshape(bsz, L, DSA_W), lp['w_br_dsa'])
              + gates[:, :, 3] * br(y_ssd, lp['w_br_ssd']))
    out = jnp.einsum('bld,de->ble', merged, lp['w_out'])
    return out.astype(dt_), (fk, fv, flogf, dk, dv, ik, gla_s, ssd_s, conv_new)


def trunk(h, pos, lead, layers, norm_final, pasts):
    states = []
    for l in range(DEPTH):
        lp = layers[l]
        past = None if pasts is None else pasts[l]
        a, st = mix_layer(rmsnorm(h, lp['norm_mix']), pos, lead, lp, past)
        h = h + a
        m = rmsnorm(h, lp['norm_mlp'])
        hid = jnp.square(jax.nn.relu(jnp.einsum('bld,df->blf', m, lp['w_up'])))
        h = h + jnp.einsum('blf,fd->bld', hid, lp['w_down'])
        states.append(st)
    stacked = tuple(jnp.stack(col) for col in zip(*states))
    return rmsnorm(h, norm_final), stacked


def setup_inputs(seed: int = 0) -> dict:
    key = jax.random.key(seed)
    ks = iter(jax.random.split(key, 48))
    f32 = jnp.float32
    def nrm(shape, scale):
        return jax.random.normal(next(ks), shape, f32) * scale
    n_pages = PAST_LEN // PAGE_SIZE
    n_pool = (DEC_BATCH * n_pages * 5) // 4
    perm = jax.random.permutation(next(ks), n_pool)
    page_table = perm[:DEC_BATCH * n_pages].reshape(DEC_BATCH, n_pages).astype(jnp.int32)
    dt0 = jnp.exp(jax.random.uniform(next(ks), (DEPTH, SSD_HEADS), f32, jnp.log(1e-3), jnp.log(1e-1)))
    return {
        'x_prompt': nrm((BATCH, SEQ, D_MODEL), 1.0),
        'x_sample': nrm((DEC_BATCH, DEC_SEQ, D_MODEL), 1.0),
        'cache_fox_k': nrm((DEPTH, n_pool, PAGE_SIZE, FOX_HEADS, FOX_HD), 1.0),
        'cache_fox_v': nrm((DEPTH, n_pool, PAGE_SIZE, FOX_HEADS, FOX_HD), 1.0),
        'cache_fox_logf': jax.nn.log_sigmoid(nrm((DEPTH, n_pool, PAGE_SIZE, FOX_HEADS), 1.0)),
        'cache_dsa_k': nrm((DEPTH, n_pool, PAGE_SIZE, DSA_HEADS, DSA_HD), 1.0),
        'cache_dsa_v': nrm((DEPTH, n_pool, PAGE_SIZE, DSA_HEADS, DSA_HD), 1.0),
        'cache_dsa_kidx': nrm((DEPTH, n_pool, PAGE_SIZE, IDX_DIM), 1.0),
        'state_gla': nrm((DEPTH, DEC_BATCH, GLA_HEADS, GLA_DK, GLA_DV), 0.5),
        'state_ssd': nrm((DEPTH, DEC_BATCH, SSD_HEADS, SSD_HD, SSD_STATE), 0.5),
        'state_ssd_conv': nrm((DEPTH, DEC_BATCH, SSD_CONV - 1, SSD_XBC), 1.0),
        'page_table': page_table,
        'meta': nrm((N_META, D_MODEL), 1.0),
        'w_in': nrm((DEPTH, D_MODEL, IN_COLS), D_MODEL ** -0.5),
        'b_in': nrm((DEPTH, IN_COLS), 0.02),
        'norm_mix': 1.0 + nrm((DEPTH, D_MODEL), 0.01),
        'gla_w_a2': nrm((DEPTH, GLA_RANK, GLA_KW), GLA_RANK ** -0.5),
        'gla_b_a2': nrm((DEPTH, GLA_KW), 0.02),
        'gla_norm': 1.0 + nrm((DEPTH, GLA_DV), 0.01),
        'ssd_conv_w': nrm((DEPTH, SSD_CONV, SSD_XBC), SSD_CONV ** -0.5),
        'ssd_conv_b': nrm((DEPTH, SSD_XBC), 0.02),
        'ssd_dt_bias': dt0 + jnp.log(-jnp.expm1(-dt0)),
        'ssd_a_log': jnp.log(jax.random.uniform(next(ks), (DEPTH, SSD_HEADS), f32, 1.0, 16.0)),
        'ssd_d': 1.0 + nrm((DEPTH, SSD_HEADS), 0.01),
        'ssd_norm': 1.0 + nrm((DEPTH, SSD_W), 0.01),
        'w_br_fox': nrm((DEPTH, FOX_W, D_MODEL), FOX_W ** -0.5),
        'w_br_gla': nrm((DEPTH, GLA_VW, D_MODEL), GLA_VW ** -0.5),
        'w_br_dsa': nrm((DEPTH, DSA_W, D_MODEL), DSA_W ** -0.5),
        'w_br_ssd': nrm((DEPTH, SSD_W, D_MODEL), SSD_W ** -0.5),
        'w_out': nrm((DEPTH, D_MODEL, D_MODEL), D_MODEL ** -0.5),
        'norm_mlp': 1.0 + nrm((DEPTH, D_MODEL), 0.01),
        'w_up': nrm((DEPTH, D_MODEL, D_FF), D_MODEL ** -0.5),
        'w_down': nrm((DEPTH, D_FF, D_MODEL), D_FF ** -0.5),
        'norm_final': 1.0 + nrm((D_MODEL,), 0.01),
    }


def reference(x_prompt, x_sample, cache_fox_k, cache_fox_v, cache_fox_logf, cache_dsa_k, cache_dsa_v,
              cache_dsa_kidx, state_gla, state_ssd, state_ssd_conv, page_table, meta, w_in, b_in, norm_mix,
              gla_w_a2, gla_b_a2, gla_norm, ssd_conv_w, ssd_conv_b, ssd_dt_bias, ssd_a_log, ssd_d, ssd_norm,
              w_br_fox, w_br_gla, w_br_dsa, w_br_ssd, w_out, norm_mlp, w_up, w_down, norm_final):
    layers = [dict(w_in=w_in[l], b_in=b_in[l], norm_mix=norm_mix[l], gla_w_a2=gla_w_a2[l], gla_b_a2=gla_b_a2[l],
                   gla_norm=gla_norm[l], ssd_conv_w=ssd_conv_w[l], ssd_conv_b=ssd_conv_b[l],
                   ssd_dt_bias=ssd_dt_bias[l], ssd_a_log=ssd_a_log[l], ssd_d=ssd_d[l], ssd_norm=ssd_norm[l],
                   w_br_fox=w_br_fox[l], w_br_gla=w_br_gla[l], w_br_dsa=w_br_dsa[l], w_br_ssd=w_br_ssd[l],
                   w_out=w_out[l], norm_mlp=norm_mlp[l], w_up=w_up[l], w_down=w_down[l])
              for l in range(DEPTH)]
    bsz = x_prompt.shape[0]
    meta_rows = jnp.broadcast_to(meta.astype(x_prompt.dtype)[None], (bsz, N_META, D_MODEL))
    hp = jnp.concatenate([meta_rows, x_prompt], axis=1)
    pos_p = jnp.arange(hp.shape[1], dtype=jnp.int32)
    yp, st_p = trunk(hp, pos_p, N_META, layers, norm_final, None)
    y_prompt = yp[:, N_META:]
    past_len = page_table.shape[1] * cache_fox_k.shape[2]
    pasts = [dict(fox_k=gather_pages(cache_fox_k[l], page_table),
                  fox_v=gather_pages(cache_fox_v[l], page_table),
                  fox_logf=gather_pages(cache_fox_logf[l], page_table),
                  dsa_k=gather_pages(cache_dsa_k[l], page_table),
                  dsa_v=gather_pages(cache_dsa_v[l], page_table),
                  dsa_kidx=gather_pages(cache_dsa_kidx[l], page_table),
                  gla=state_gla[l], ssd=state_ssd[l], ssd_conv=state_ssd_conv[l])
             for l in range(DEPTH)]
    pos_s = past_len + jnp.arange(x_sample.shape[1], dtype=jnp.int32)
    y_sample, st_s = trunk(x_sample, pos_s, 0, layers, norm_final, pasts)
    p_fk, p_fv, p_fl, p_dk, p_dv, p_ki, p_gla, p_ssd, p_conv = st_p
    s_fk, s_fv, s_fl, s_dk, s_dv, s_ki, s_gla, s_ssd, s_conv = st_s
    return (y_prompt, y_sample, p_fk, p_fv, p_fl, p_dk, p_dv, p_ki, p_gla, p_ssd, p_conv,
            s_fk, s_fv, s_fl, s_dk, s_dv, s_ki, s_gla, s_ssd, s_conv)
```

```python
import functools
from typing import NamedTuple

import numpy as np
import jax
import jax.numpy as jnp
from jax import lax
from jax.experimental import pallas as pl
from jax.experimental.pallas import tpu as pltpu

F32 = jnp.float32
BF16 = jnp.bfloat16
I32 = jnp.int32

D_MODEL = 1024
N_META = 16
PAGE_SIZE = 128
CHUNK = 64
ROPE_THETA = 10000.0
EPS = 1e-6
NEG = -1e30
HEADS = 4
HD = 64
GLA_DK = 32
GLA_TAU = 16.0
DSA_TOPK = 256
SSD_CONV = 4
D_FF = 4 * D_MODEL

LANES = 128
VMEM_LIMIT = 56 * 1024 * 1024

_IN_SPLITS = (('fox_q', 256), ('fox_k', 256), ('fox_v', 256), ('fox_f', 4),
              ('gla_q', 128), ('gla_k', 128), ('gla_v', 256), ('gla_a', 16), ('gla_r', 256),
              ('dsa_q', 256), ('dsa_k', 256), ('dsa_v', 256),
              ('idx_q', 256), ('idx_w', 4), ('idx_k', 64),
              ('ssd_z', 256), ('ssd_xbc', 512), ('ssd_dt', 4),
              ('gates', 4096))
_COL = dict(gates=(0, 4096), ssd_xbc=(4096, 512), fox_q=(4608, 256), fox_k=(4864, 256), fox_v=(5120, 256),
            dsa_q=(5376, 256), dsa_k=(5632, 256), dsa_v=(5888, 256), idx_q=(6144, 256), ssd_z=(6400, 256),
            gla_v=(6656, 256), gla_r=(6912, 256), gla_q=(7168, 128), gla_k=(7296, 128),
            small=(7424, 128), idx_k=(7552, 128))
NP_COLS = 7680
_SMALL = dict(fox_f=0, idx_w=4, ssd_dt=8, gla_a=16)


def _cblk(name):
    off, w = _COL[name]
    assert off % w == 0
    return off // w, w


class Geom(NamedTuple):
    B: int
    Lm: int
    Lt: int
    Lpast: int
    tail_row0: int

    @property
    def km(self):
        return self.Lm if self.Lm else self.Lpast

    @property
    def pos_main0(self):
        return self.Lt if self.Lm else 0

    @property
    def pos_tail0(self):
        return 0 if self.Lm else self.Lpast

    @property
    def tail_blk0(self):
        assert self.tail_row0 % self.Lt == 0
        return self.tail_row0 // self.Lt


def _main_spec(g, w, cblk=0):
    return pl.BlockSpec((g.Lm, w), lambda b, *_: (b, cblk))


def _tail_spec(g, w, cblk=0):
    return pl.BlockSpec((g.Lt, w), lambda b, *_: (g.tail_blk0 + b, cblk))


def _params(sem):
    return pltpu.CompilerParams(dimension_semantics=sem, vmem_limit_bytes=VMEM_LIMIT)


def _log_sigmoid(x):
    return jnp.minimum(x, 0.0) - jnp.log1p(jnp.exp(-jnp.abs(x)))


def _softplus(x):
    return jnp.maximum(x, 0.0) + jnp.log1p(jnp.exp(-jnp.abs(x)))


def _silu(x):
    return x * jax.nn.sigmoid(x)


def _split3(x):
    hi = x.astype(BF16)
    r1 = x - hi.astype(F32)
    mid = r1.astype(BF16)
    lo = (r1 - mid.astype(F32)).astype(BF16)
    return hi, mid, lo


def _tri_left(tri, x):
    hi, mid, lo = _split3(x)
    d = functools.partial(jnp.dot, preferred_element_type=F32)
    return d(tri, hi) + d(tri, mid) + d(tri, lo)


def _tri_right(x, tri):
    hi, mid, lo = _split3(x)
    d = functools.partial(jnp.dot, preferred_element_type=F32)
    return d(hi, tri) + d(mid, tri) + d(lo, tri)


def _lower_tri(n, dtype=BF16):
    r = lax.broadcasted_iota(I32, (n, n), 0)
    c = lax.broadcasted_iota(I32, (n, n), 1)
    return (r >= c).astype(dtype)


def _upper_tri(n, dtype=BF16):
    r = lax.broadcasted_iota(I32, (n, n), 0)
    c = lax.broadcasted_iota(I32, (n, n), 1)
    return (r <= c).astype(dtype)


def _dot(a, b):
    return jnp.dot(a, b, preferred_element_type=F32)


def _dot_nt(a, b):
    return lax.dot_general(a, b, (((1,), (1,)), ((), ())), preferred_element_type=F32)


def _dot_tn(a, b):
    return lax.dot_general(a, b, (((0,), (0,)), ((), ())), preferred_element_type=F32)


def _head_of_lane(shape, width):
    return lax.broadcasted_iota(I32, shape, len(shape) - 1) // width


def _rms(x, g):
    ms = jnp.mean(x * x, axis=-1, keepdims=True)
    return x * lax.rsqrt(ms + EPS) * g


def _proj_body(x_ref, g_ref, w_ref, b_ref, o_ref):
    u = _rms(x_ref[...], g_ref[...]).astype(BF16)
    o_ref[...] = _dot(u, w_ref[...]) + b_ref[...]


def _proj(h, g, w, b, tm, tn):
    T, D = h.shape
    N = w.shape[1]
    return pl.pallas_call(
        _proj_body,
        grid=(N // tn, T // tm),
        in_specs=[pl.BlockSpec((tm, D), lambda j, i: (i, 0)),
                  pl.BlockSpec((1, D), lambda j, i: (0, 0)),
                  pl.BlockSpec((D, tn), lambda j, i: (0, j)),
                  pl.BlockSpec((1, tn), lambda j, i: (0, j))],
        out_specs=pl.BlockSpec((tm, tn), lambda j, i: (i, j)),
        out_shape=jax.ShapeDtypeStruct((T, N), F32),
        compiler_params=_params(("arbitrary", "arbitrary")),
        name="proj",
    )(h, g, w, b)


def _rope128(x, cos, sin_signed):
    lane = lax.broadcasted_iota(I32, x.shape, 1)
    swapped = jnp.where((lane % HD) < HD // 2, pltpu.roll(x, LANES - HD // 2, 1), pltpu.roll(x, HD // 2, 1))
    return x * cos + swapped * sin_signed


def _prep_body(dq_ref, dk_ref, iq_ref, ik_ref, sm_ref, cos_ref, sin_ref, odq, odk, oiq, oik, olf):
    cos = cos_ref[...]
    sin = sin_ref[...]
    for src, dst in ((dq_ref, odq), (dk_ref, odk), (iq_ref, oiq)):
        for half in range(2):
            sl = slice(half * LANES, (half + 1) * LANES)
            dst[:, sl] = _rope128(src[:, sl], cos, sin)
    oik[...] = _rope128(ik_ref[...], cos, sin)
    olf[...] = _log_sigmoid(sm_ref[...])


def _prep(pr, cos, sin, tm):
    T = pr.shape[0]

    def col(name):
        cb, w = _cblk(name)
        return pl.BlockSpec((tm, w), lambda i: (i, cb))
    row = lambda w: pl.BlockSpec((tm, w), lambda i: (i, 0))
    return pl.pallas_call(
        _prep_body,
        grid=(T // tm,),
        in_specs=[col('dsa_q'), col('dsa_k'), col('idx_q'), col('idx_k'), col('small'), row(LANES), row(LANES)],
        out_specs=[row(256), row(256), row(256), row(LANES), row(LANES)],
        out_shape=[jax.ShapeDtypeStruct((T, 256), F32)] * 3 + [jax.ShapeDtypeStruct((T, LANES), F32)] * 2,
        compiler_params=_params(("arbitrary",)),
        name="prep",
    )(pr, pr, pr, pr, pr, cos, sin)


def _merge_body(yf, yg, yd, ys, gates_ref, h_ref, wbr_ref, wout_ref, o_ref):
    merged = None
    for b, y in enumerate((yf, yg, yd, ys)):
        gate = jax.nn.sigmoid(gates_ref[:, b * D_MODEL:(b + 1) * D_MODEL])
        term = gate * _dot(y[...].astype(BF16), wbr_ref[b])
        merged = term if merged is None else merged + term
    o_ref[...] = h_ref[...] + _dot(merged.astype(BF16), wout_ref[...])


def _merge(ys, pr, h, wbr, wout, tm):
    T = h.shape[0]
    row = lambda w: pl.BlockSpec((tm, w), lambda i: (i, 0))
    return pl.pallas_call(
        _merge_body,
        grid=(T // tm,),
        in_specs=[row(256)] * 4 + [row(4 * D_MODEL), row(D_MODEL),
                                   pl.BlockSpec((4, 256, D_MODEL), lambda i: (0, 0, 0)),
                                   pl.BlockSpec((D_MODEL, D_MODEL), lambda i: (0, 0))],
        out_specs=row(D_MODEL),
        out_shape=jax.ShapeDtypeStruct((T, D_MODEL), F32),
        compiler_params=_params(("arbitrary",)),
        name="merge",
    )(*ys, pr, h, wbr, wout)


def _mlp_body(with_final, h_ref, g_ref, wup_ref, wdn_ref, gf_ref, o_ref, *maybe_y):
    h = h_ref[...]
    m = _rms(h, g_ref[...]).astype(BF16)
    acc = h
    for c in range(D_FF // D_MODEL):
        sl = slice(c * D_MODEL, (c + 1) * D_MODEL)
        hid = jnp.square(jnp.maximum(_dot(m, wup_ref[:, sl]), 0.0))
        acc = acc + _dot(hid.astype(BF16), wdn_ref[sl, :])
    o_ref[...] = acc
    if with_final:
        maybe_y[0][...] = _rms(acc, gf_ref[...])


def _mlp(h, g, wup, wdn, gf, with_final, tm):
    T = h.shape[0]
    row = pl.BlockSpec((tm, D_MODEL), lambda i: (i, 0))
    vec = pl.BlockSpec((1, D_MODEL), lambda i: (0, 0))
    n_out = 2 if with_final else 1
    outs = pl.pallas_call(
        functools.partial(_mlp_body, with_final),
        grid=(T // tm,),
        in_specs=[row, vec,
                  pl.BlockSpec((D_MODEL, D_FF), lambda i: (0, 0), pipeline_mode=pl.Buffered(1)),
                  pl.BlockSpec((D_FF, D_MODEL), lambda i: (0, 0), pipeline_mode=pl.Buffered(1)),
                  vec],
        out_specs=[row] * n_out,
        out_shape=[jax.ShapeDtypeStruct((T, D_MODEL), F32)] * n_out,
        compiler_params=_params(("arbitrary",)),
        name="mlp",
    )(h, g, wup, wdn, gf)
    return outs


def _gather_body(layer, n_seq, n_pages, n_t, pt_ref, *refs):
    pools = refs[:n_t]
    outs = refs[n_t:2 * n_t]
    sem = refs[2 * n_t]

    def copies(b):
        slot = b % 2
        for t in range(n_t):
            for p in range(n_pages):
                src = pools[t].at[layer, pt_ref[b, p]]
                nd = len(src.shape)
                idx = (b,) + (slice(None),) * (nd - 1) + (pl.ds(p * PAGE_SIZE, PAGE_SIZE),)
                yield pltpu.make_async_copy(src, outs[t].at[idx], sem.at[slot, t, p])

    def start(b):
        for c in copies(b):
            c.start()

    def wait(b):
        for c in copies(b):
            c.wait()

    start(0)

    def step(b, carry):
        start(b)
        wait(b - 1)
        return carry
    lax.fori_loop(1, n_seq, step, 0)
    wait(n_seq - 1)


def _gather_pages(layer, page_table, pools):
    n_seq, n_pages = page_table.shape
    n_t = len(pools)
    out_shape = [jax.ShapeDtypeStruct((n_seq,) + p.shape[2:-1] + (n_pages * PAGE_SIZE,), p.dtype) for p in pools]
    any_spec = pl.BlockSpec(memory_space=pl.ANY)
    return pl.pallas_call(
        functools.partial(_gather_body, layer, n_seq, n_pages, n_t),
        in_specs=[pl.BlockSpec(memory_space=pltpu.SMEM)] + [any_spec] * n_t,
        out_specs=[any_spec] * n_t,
        out_shape=out_shape,
        scratch_shapes=[pltpu.SemaphoreType.DMA((2, n_t, n_pages))],
        name="gather_pages",
    )(page_table, *pools)


def _foxc_body(g, *refs):
    if g.Lm:
        lfm, lft, ocm, oct_, orm, ort, pad = refs
    else:
        lfm, lft, oct_, orm, ort, pad = refs
    nblk = g.km // LANES
    tl = _lower_tri(LANES)
    ort[...] = jnp.zeros_like(ort)
    pad[...] = jnp.zeros_like(pad)
    if g.Lm:
        ct = _tri_left(_lower_tri(g.Lt), lft[...])
        oct_[...] = ct
        pad[0:g.Lt, :] = ct
        ort[...] = pad[...].T[0:8, :]
        carry = ct[g.Lt - 1:g.Lt, :]
        for j in range(nblk):
            sl = slice(j * LANES, (j + 1) * LANES)
            c = _tri_left(tl, lfm[sl, :]) + carry
            ocm[sl, :] = c
            orm[:, sl] = c.T[0:8, :]
            carry = c[LANES - 1:LANES, :]
    else:
        tu = _upper_tri(LANES)
        carry = jnp.zeros((8, 1), F32)
        c = None
        for j in range(nblk):
            sl = slice(j * LANES, (j + 1) * LANES)
            c = _tri_right(lfm[:, sl], tu) + carry
            orm[:, sl] = c
            carry = c[:, LANES - 1:LANES]
        pad[0:8, :] = c
        carry_row = pad[...].T[LANES - 1:LANES, :]
        ct = _tri_left(_lower_tri(g.Lt), lft[...]) + carry_row
        oct_[...] = ct
        pad[0:8, :] = jnp.zeros((8, LANES), F32)
        pad[0:g.Lt, :] = ct
        ort[...] = pad[...].T[0:8, :]


def _fox_c(g, lf_main, lf_tail):
    B = g.B
    outs_shape, outs_spec = [], []
    if g.Lm:
        in_specs = [_main_spec(g, LANES), _tail_spec(g, LANES)]
        outs_shape.append(jax.ShapeDtypeStruct((B * g.Lm, LANES), F32))
        outs_spec.append(pl.BlockSpec((g.Lm, LANES), lambda b: (b, 0)))
    else:
        in_specs = [pl.BlockSpec((None, 8, g.km), lambda b: (b, 0, 0)), _tail_spec(g, LANES)]
    outs_shape += [jax.ShapeDtypeStruct((B * g.Lt, LANES), F32),
                   jax.ShapeDtypeStruct((B, 8, g.km), F32),
                   jax.ShapeDtypeStruct((B, 8, LANES), F32)]
    outs_spec += [pl.BlockSpec((g.Lt, LANES), lambda b: (b, 0)),
                  pl.BlockSpec((None, 8, g.km), lambda b: (b, 0, 0)),
                  pl.BlockSpec((None, 8, LANES), lambda b: (b, 0, 0))]
    return pl.pallas_call(
        functools.partial(_foxc_body, g),
        grid=(B,),
        in_specs=in_specs,
        out_specs=outs_spec,
        out_shape=outs_shape,
        scratch_shapes=[pltpu.VMEM((LANES, LANES), F32)],
        compiler_params=_params(("arbitrary",)),
        name="fox_c",
    )(lf_main, lf_tail)


def _sort_key(x):
    i = lax.bitcast_convert_type(x + 0.0, I32)
    return i ^ ((i >> 31) & jnp.int32(0x7FFFFFFF))


def _count(mask_m, mask_t):
    return (jnp.sum(mask_m.astype(I32), axis=-1, keepdims=True)
            + jnp.sum(mask_t.astype(I32), axis=-1, keepdims=True))


def _topk_mask(sc_m, sc_t, kpos_m, kpos_t, topk, max_pos):
    key_m = _sort_key(sc_m)
    key_t = _sort_key(sc_t)
    int_min = jnp.int32(-2 ** 31)

    n0 = _count(key_m >= 0, key_t >= 0)
    t0 = jnp.where(n0 >= topk, jnp.int32(0), int_min)

    def t_step(i, t):
        cand = t + lax.shift_left(jnp.int32(1), 30 - i)
        n = _count(key_m >= cand, key_t >= cand)
        return jnp.where(n >= topk, cand, t)
    t = lax.fori_loop(0, 31, t_step, t0)

    gt_m, gt_t = key_m > t, key_t > t
    tie_m, tie_t = key_m == t, key_t == t
    need = topk - _count(gt_m, gt_t)

    nbits = int(max_pos).bit_length()

    def p_step(i, p):
        cand = p + lax.shift_left(jnp.int32(1), nbits - 1 - i)
        n = _count(tie_m & (kpos_m < cand), tie_t & (kpos_t < cand))
        return jnp.where(n < need, cand, p)
    p = lax.fori_loop(0, nbits, p_step, jnp.zeros_like(need))
    return gt_m | (tie_m & (kpos_m <= p)), gt_t | (tie_t & (kpos_t <= p))


def _attn_body(kind, g, topk, *refs):
    refs = list(refs)
    take = lambda n: [refs.pop(0) for _ in range(n)]
    has_q_main = bool(g.Lm)
    R = {}
    R['qt'], = take(1)
    if has_q_main:
        R['qm'], = take(1)
    R['kt'], R['vt'], R['km'], R['vm'] = take(4)
    if kind == 'fox':
        R['cqt'], = take(1)
        if has_q_main:
            R['cqm'], = take(1)
        R['crm'], R['crt'] = take(2)
    else:
        R['iqt'], R['wt'] = take(2)
        if has_q_main:
            R['iqm'], R['wm'] = take(2)
        R['ikt'], R['ikm'] = take(2)
    R['ot'], = take(1)
    if has_q_main:
        R['om'], = take(1)
    R['KT'], R['VT'], R['ktp'], R['vtp'] = take(4)
    if kind == 'dsa':
        R['IKT'], R['iktp'] = take(2)
    assert not refs

    KM, Lt = g.km, g.Lt
    i = pl.program_id(1)

    @pl.when(i == 0)
    def _init():
        if has_q_main:
            for j in range(KM // 256):
                sl = slice(j * 256, (j + 1) * 256)
                kT = R['km'][sl, :].T.astype(BF16)
                vT = R['vm'][sl, :].T.astype(BF16)
                for h in range(HEADS):
                    R['KT'][h, :, sl] = kT[h * HD:(h + 1) * HD, :]
                    R['VT'][h, :, sl] = vT[h * HD:(h + 1) * HD, :]
            if kind == 'dsa':
                for j in range(KM // LANES):
                    sl = slice(j * LANES, (j + 1) * LANES)
                    R['IKT'][:, sl] = R['ikm'][sl, :].T[0:HD, :].astype(BF16)
        else:
            R['KT'][...] = R['km'][...].astype(BF16)
            R['VT'][...] = R['vm'][...].astype(BF16)
            if kind == 'dsa':
                R['IKT'][...] = R['ikm'][...].astype(BF16)
        R['ktp'][...] = jnp.zeros_like(R['ktp'])
        R['vtp'][...] = jnp.zeros_like(R['vtp'])
        R['ktp'][0:Lt, :] = R['kt'][...].astype(BF16)
        R['vtp'][0:Lt, :] = R['vt'][...].astype(BF16)
        if kind == 'dsa':
            R['iktp'][...] = jnp.zeros_like(R['iktp'])
            R['iktp'][0:Lt, :] = R['ikt'][...].astype(BF16)

    def process(rows, q, extra, qpos0, out_ref, out_rows):
        qpos = qpos0 + lax.broadcasted_iota(I32, (rows, 1), 0)
        kpos_m = g.pos_main0 + lax.broadcasted_iota(I32, (1, KM), 1)
        lane_t = lax.broadcasted_iota(I32, (1, LANES), 1)
        kpos_t = g.pos_tail0 + lane_t
        mask_m = kpos_m <= qpos
        mask_t = (kpos_t <= qpos) & (lane_t < Lt)
        if kind == 'dsa':
            iq, w = extra
            sc_m = jnp.zeros((rows, KM), F32)
            sc_t = jnp.zeros((rows, LANES), F32)
            for h in range(HEADS):
                iqh = iq[:, h * HD:(h + 1) * HD].astype(BF16)
                wh = w[:, _SMALL['idx_w'] + h:_SMALL['idx_w'] + h + 1]
                sc_m = sc_m + wh * jnp.maximum(_dot(iqh, R['IKT'][...]), 0.0)
                sc_t = sc_t + wh * jnp.maximum(_dot_nt(iqh, R['iktp'][:, 0:HD]), 0.0)
            sc_m = jnp.where(mask_m, sc_m, NEG)
            sc_t = jnp.where(lane_t < Lt, jnp.where(mask_t, sc_t, NEG), -jnp.inf)
            sel_m, sel_t = _topk_mask(sc_m, sc_t, kpos_m, kpos_t, topk, g.km + g.Lt)
            mask_m = mask_m & sel_m
            mask_t = mask_t & sel_t
        else:
            cq, = extra
        for h in range(HEADS):
            hs = slice(h * HD, (h + 1) * HD)
            qh = q[:, hs].astype(BF16)
            s_m = _dot(qh, R['KT'][h]) * (HD ** -0.5)
            s_t = _dot_nt(qh, R['ktp'][:, hs]) * (HD ** -0.5)
            if kind == 'fox':
                cqh = cq[:, h:h + 1]
                s_m = s_m + cqh - R['crm'][h:h + 1, :]
                s_t = s_t + cqh - R['crt'][h:h + 1, :]
            s_m = jnp.where(mask_m, s_m, NEG)
            s_t = jnp.where(mask_t, s_t, NEG)
            mx = jnp.maximum(jnp.max(s_m, axis=-1, keepdims=True), jnp.max(s_t, axis=-1, keepdims=True))
            p_m = jnp.exp(s_m - mx)
            p_t = jnp.exp(s_t - mx)
            den = jnp.sum(p_m, axis=-1, keepdims=True) + jnp.sum(p_t, axis=-1, keepdims=True)
            o = _dot_nt(p_m.astype(BF16), R['VT'][h]) + _dot(p_t.astype(BF16), R['vtp'][:, hs])
            out_ref[out_rows, hs] = o / den

    def extras(suffix, rows_sl):
        if kind == 'fox':
            return (R['cq' + suffix][rows_sl, :],)
        return (R['iq' + suffix][rows_sl, :], R['w' + suffix][rows_sl, :])

    @pl.when(i == 0)
    def _tail_queries():
        all_rows = slice(None)
        process(Lt, R['qt'][...], extras('t', all_rows), g.pos_tail0, R['ot'], all_rows)

    if has_q_main:
        @pl.when(i > 0)
        def _main_queries():
            r0 = pl.multiple_of((i - 1) * LANES, LANES)
            rows_sl = pl.ds(r0, LANES)
            process(LANES, R['qm'][rows_sl, :], extras('m', rows_sl), g.pos_main0 + r0, R['om'], rows_sl)


def _attention(kind, g, q, k, v, kv_main, fox=None, dsa=None):
    B, KM, Lt = g.B, g.km, g.Lt
    has_q_main = bool(g.Lm)
    nq = 1 + (g.Lm // LANES if has_q_main else 0)
    ins, specs = [], []

    def add(arr, spec):
        ins.append(arr)
        specs.append(spec)

    def add_tok(entry, w, main=True, tail=True):
        arr_m, arr_t, cb = entry
        if tail:
            add(arr_t, _tail_spec(g, w, cb))
        if main and has_q_main:
            add(arr_m, _main_spec(g, w, cb))

    add_tok(q, 256)
    add_tok(k, 256, main=False)
    add_tok(v, 256, main=False)
    if has_q_main:
        add(k[0], _main_spec(g, 256, k[2]))
        add(v[0], _main_spec(g, 256, v[2]))
    else:
        for a in kv_main:
            add(a, pl.BlockSpec((None, HEADS, HD, KM), lambda b, i: (b, 0, 0, 0)))
    if kind == 'fox':
        c_main_col, c_tail_col, c_main_row, c_tail_row = fox
        add(c_tail_col, pl.BlockSpec((Lt, LANES), lambda b, i: (b, 0)))
        if has_q_main:
            add(c_main_col, pl.BlockSpec((g.Lm, LANES), lambda b, i: (b, 0)))
        add(c_main_row, pl.BlockSpec((None, 8, KM), lambda b, i: (b, 0, 0)))
        add(c_tail_row, pl.BlockSpec((None, 8, LANES), lambda b, i: (b, 0, 0)))
    else:
        iq, w, ik, ikT_past = dsa
        add(iq[1], _tail_spec(g, 256, iq[2]))
        add(w[1], _tail_spec(g, LANES, w[2]))
        if has_q_main:
            add(iq[0], _main_spec(g, 256, iq[2]))
            add(w[0], _main_spec(g, LANES, w[2]))
        add(ik[1], _tail_spec(g, LANES, ik[2]))
        if has_q_main:
            add(ik[0], _main_spec(g, LANES, ik[2]))
        else:
            add(ikT_past, pl.BlockSpec((None, HD, KM), lambda b, i: (b, 0, 0)))

    out_shape = [jax.ShapeDtypeStruct((B * Lt, 256), F32)]
    out_specs = [pl.BlockSpec((Lt, 256), lambda b, i: (b, 0))]
    if has_q_main:
        out_shape.append(jax.ShapeDtypeStruct((B * g.Lm, 256), F32))
        out_specs.append(pl.BlockSpec((g.Lm, 256), lambda b, i: (b, 0)))
    scratch = [pltpu.VMEM((HEADS, HD, KM), BF16), pltpu.VMEM((HEADS, HD, KM), BF16),
               pltpu.VMEM((LANES, 256), BF16), pltpu.VMEM((LANES, 256), BF16)]
    if kind == 'dsa':
        scratch += [pltpu.VMEM((HD, KM), BF16), pltpu.VMEM((LANES, LANES), BF16)]
    topk = min(DSA_TOPK, (KM + Lt) // 4)
    outs = pl.pallas_call(
        functools.partial(_attn_body, kind, g, topk),
        grid=(B, nq),
        in_specs=specs,
        out_specs=out_specs,
        out_shape=out_shape,
        scratch_shapes=scratch,
        compiler_params=_params(("arbitrary", "arbitrary")),
        name=kind + "_attn",
    )(*ins)
    return outs


def _gla_chunk(c, q, k, v, r, small, S, wa2, ba2, gnorm):
    logit = _dot(small.astype(BF16), wa2) + ba2
    gate = _log_sigmoid(logit) / GLA_TAU
    b = _tri_left(_lower_tri(c), gate)
    bT = b.T
    kT = k.T
    qs = q * (GLA_DK ** -0.5)
    row = lax.broadcasted_iota(I32, (c, c), 0)
    col = lax.broadcasted_iota(I32, (c, c), 1)
    causal = row >= col
    head256 = _head_of_lane((c, 256), 64)
    vb = v.astype(BF16)
    y = _dot((qs * jnp.exp(b)).astype(BF16), S.astype(BF16))
    for h in range(HEADS):
        att = jnp.zeros((c, c), F32)
        for d in range(GLA_DK):
            j = h * GLA_DK + d
            decay = jnp.exp(jnp.minimum(b[:, j:j + 1] - bT[j:j + 1, :], 0.0))
            att = att + (qs[:, j:j + 1] * kT[j:j + 1, :]) * decay
        att = jnp.where(causal, att, 0.0)
        y = y + jnp.where(head256 == h, _dot(att.astype(BF16), vb), 0.0)
    b_last_row = b[c - 1:c, :]
    b_last_col = bT[:, c - 1:c]
    kd = (k * jnp.exp(b_last_row - b)).astype(BF16)
    upd = _dot_tn(kd, vb)
    diag = (lax.broadcasted_iota(I32, (LANES, 256), 0) // GLA_DK) == _head_of_lane((LANES, 256), 64)
    S_new = jnp.exp(b_last_col) * S + jnp.where(diag, upd, 0.0)
    inv = jnp.zeros((c, 256), F32)
    for h in range(HEADS):
        ms = jnp.sum(jnp.where(head256 == h, y * y, 0.0), axis=-1, keepdims=True) / 64.0
        inv = jnp.where(head256 == h, lax.rsqrt(ms + EPS), inv)
    out = (y * inv * gnorm) * _silu(r)
    return out, S_new


def _gla_body(g, *refs):
    refs = list(refs)
    take = lambda n: [refs.pop(0) for _ in range(n)]
    qt, kt, vt, rt, st = take(5)
    if g.Lm:
        qm, km, vm, rm, sm = take(5)
    else:
        s0, = take(1)
    wa2, ba2, gnorm = take(3)
    yt, = take(1)
    if g.Lm:
        ym, = take(1)
    s_out, = take(1)
    assert not refs
    w = wa2[...]
    bb = ba2[...]
    gn = gnorm[...]
    S = jnp.zeros((LANES, 256), F32) if g.Lm else s0[...]
    out, S = _gla_chunk(g.Lt, qt[...], kt[...], vt[...], rt[...], st[...], S, w, bb, gn)
    yt[...] = out
    if g.Lm:
        s_out[...] = S

        def step(ci, carry):
            sl = pl.ds(pl.multiple_of(ci * CHUNK, CHUNK), CHUNK)
            o, s_new = _gla_chunk(CHUNK, qm[sl, :], km[sl, :], vm[sl, :], rm[sl, :], sm[sl, :], s_out[...], w, bb, gn)
            ym[sl, :] = o
            s_out[...] = s_new
            return carry
        lax.fori_loop(0, g.Lm // CHUNK, step, 0)
    else:
        s_out[...] = S


def _gla(g, pr_m, pr_t, s0, wa2, ba2, gnorm):
    B = g.B
    ins, specs = [], []
    for name in ('gla_q', 'gla_k', 'gla_v', 'gla_r', 'small'):
        cb, w = _cblk(name)
        ins.append(pr_t)
        specs.append(_tail_spec(g, w, cb))
    if g.Lm:
        for name in ('gla_q', 'gla_k', 'gla_v', 'gla_r', 'small'):
            cb, w = _cblk(name)
            ins.append(pr_m)
            specs.append(_main_spec(g, w, cb))
    else:
        ins.append(s0)
        specs.append(pl.BlockSpec((None, LANES, 256), lambda b: (b, 0, 0)))
    ins += [wa2, ba2, gnorm]
    specs += [pl.BlockSpec((LANES, LANES), lambda b: (0, 0)), pl.BlockSpec((1, LANES), lambda b: (0, 0)),
              pl.BlockSpec((1, 256), lambda b: (0, 0))]
    out_shape = [jax.ShapeDtypeStruct((B * g.Lt, 256), F32)]
    out_specs = [pl.BlockSpec((g.Lt, 256), lambda b: (b, 0))]
    if g.Lm:
        out_shape.append(jax.ShapeDtypeStruct((B * g.Lm, 256), F32))
        out_specs.append(pl.BlockSpec((g.Lm, 256), lambda b: (b, 0)))
    out_shape.append(jax.ShapeDtypeStruct((B, LANES, 256), F32))
    out_specs.append(pl.BlockSpec((None, LANES, 256), lambda b: (b, 0, 0)))
    return pl.pallas_call(
        functools.partial(_gla_body, g),
        grid=(B,),
        in_specs=specs,
        out_specs=out_specs,
        out_shape=out_shape,
        compiler_params=_params(("arbitrary",)),
        name="gla",
    )(*ins)


_DT_LANE = _SMALL['ssd_dt']


def _expand_heads(x, c):
    head256 = _head_of_lane((c, 256), 64)
    out = jnp.zeros((c, 256), F32)
    for h in range(HEADS):
        out = jnp.where(head256 == h, x[:, _DT_LANE + h:_DT_LANE + h + 1], out)
    return out


def _ssd_chunk(c, xbc, z, small, H, dtb, aneg, dvec, norm):
    x = xbc[:, 0:256]
    Bm = xbc[:, 256:384]
    Cm = xbc[:, 384:512]
    dt = _softplus(small + dtb)
    a = dt * aneg
    cum = _tri_left(_lower_tri(c), a)
    cumT = cum.T
    dtT = dt.T
    row = lax.broadcasted_iota(I32, (c, c), 0)
    col = lax.broadcasted_iota(I32, (c, c), 1)
    causal = row >= col
    head256 = _head_of_lane((c, 256), 64)
    xb = x.astype(BF16)
    Cb = Cm.astype(BF16)
    Bb = Bm.astype(BF16)
    y = _dot(Cb, H.astype(BF16)) * _expand_heads(jnp.exp(cum), c)
    cb = [_dot_nt(Cb[:, gi * 64:(gi + 1) * 64], Bb[:, gi * 64:(gi + 1) * 64]) for gi in range(2)]
    for h in range(HEADS):
        j = _DT_LANE + h
        seg = jnp.exp(jnp.minimum(cum[:, j:j + 1] - cumT[j:j + 1, :], 0.0))
        m = jnp.where(causal, cb[h // 2] * seg, 0.0) * dtT[j:j + 1, :]
        y = y + jnp.where(head256 == h, _dot(m.astype(BF16), xb), 0.0)
    last_row = cum[c - 1:c, :]
    wgt = jnp.exp(last_row - cum) * dt
    xw = (x * _expand_heads(wgt, c)).astype(BF16)
    upd = _dot_tn(Bb, xw)
    same_group = (lax.broadcasted_iota(I32, (LANES, 256), 0) // 64) == (_head_of_lane((LANES, 256), 64) // 2)
    decay = _expand_heads(jnp.exp(last_row), 1)
    H_new = decay * H + jnp.where(same_group, upd, 0.0)
    sy = y + dvec * x
    gated = sy * _silu(z)
    out = _rms(gated, norm)
    return out, H_new


def _ssd_body(g, *refs):
    refs = list(refs)
    take = lambda n: [refs.pop(0) for _ in range(n)]
    xt, zt, st = take(3)
    if g.Lm:
        xm, zm, sm = take(3)
    else:
        buf0, h0 = take(2)
    cw, cb, dtb, aneg, dvec, norm = take(6)
    yt, = take(1)
    if g.Lm:
        ym, = take(1)
    h_out, = take(1)
    xin, xc = take(2)
    assert not refs
    Lt, Lm = g.Lt, g.Lm
    L = Lt + Lm
    if g.Lm:
        xin[0:8, :] = jnp.zeros((8, 512), F32)
    else:
        xin[0:8, :] = jnp.zeros((8, 512), F32)
        xin[5:8, :] = buf0[...]
    xin[8:8 + Lt, :] = xt[...]
    if g.Lm:
        xin[8 + Lt:8 + L, :] = xm[...]
    w = cw[...]
    bias = cb[...]

    def conv(blk):
        n = blk.shape[0] - 8
        acc = bias + blk[5:5 + n] * w[0:1, :]
        acc = acc + blk[6:6 + n] * w[1:2, :]
        acc = acc + blk[7:7 + n] * w[2:3, :]
        acc = acc + blk[8:8 + n] * w[3:4, :]
        return _silu(acc)

    xc[0:Lt, :] = conv(xin[0:8 + Lt, :])
    if g.Lm:
        def cstep(ci, carry):
            r0 = pl.multiple_of(Lt + ci * LANES, 8)
            xc[pl.ds(r0, LANES), :] = conv(xin[pl.ds(r0, LANES + 8), :])
            return carry
        lax.fori_loop(0, Lm // LANES, cstep, 0)

    consts = (dtb[...], aneg[...], dvec[...], norm[...])
    H = jnp.zeros((LANES, 256), F32) if g.Lm else h0[...]
    out, H = _ssd_chunk(Lt, xc[0:Lt, :], zt[...], st[...], H, *consts)
    yt[...] = out
    h_out[...] = H
    if g.Lm:
        def step(ci, carry):
            sl = pl.ds(pl.multiple_of(ci * CHUNK, CHUNK), CHUNK)
            slc = pl.ds(pl.multiple_of(Lt + ci * CHUNK, 8), CHUNK)
            o, h_new = _ssd_chunk(CHUNK, xc[slc, :], zm[sl, :], sm[sl, :], h_out[...], *consts)
            ym[sl, :] = o
            h_out[...] = h_new
            return carry
        lax.fori_loop(0, Lm // CHUNK, step, 0)


def _ssd(g, pr_m, pr_t, buf0, h0, cw, cb, dtb, aneg, dvec, norm):
    B = g.B
    ins, specs = [], []
    names = ('ssd_xbc', 'ssd_z', 'small')
    for name in names:
        cbk, w = _cblk(name)
        ins.append(pr_t)
        specs.append(_tail_spec(g, w, cbk))
    if g.Lm:
        for name in names:
            cbk, w = _cblk(name)
            ins.append(pr_m)
            specs.append(_main_spec(g, w, cbk))
    else:
        ins += [buf0, h0]
        specs += [pl.BlockSpec((None, SSD_CONV - 1, 512), lambda b: (b, 0, 0)),
                  pl.BlockSpec((None, LANES, 256), lambda b: (b, 0, 0))]
    ins += [cw, cb, dtb, aneg, dvec, norm]
    specs += [pl.BlockSpec((SSD_CONV, 512), lambda b: (0, 0)), pl.BlockSpec((1, 512), lambda b: (0, 0)),
              pl.BlockSpec((1, LANES), lambda b: (0, 0)), pl.BlockSpec((1, LANES), lambda b: (0, 0)),
              pl.BlockSpec((1, 256), lambda b: (0, 0)), pl.BlockSpec((1, 256), lambda b: (0, 0))]
    out_shape = [jax.ShapeDtypeStruct((B * g.Lt, 256), F32)]
    out_specs = [pl.BlockSpec((g.Lt, 256), lambda b: (b, 0))]
    if g.Lm:
        out_shape.append(jax.ShapeDtypeStruct((B * g.Lm, 256), F32))
        out_specs.append(pl.BlockSpec((g.Lm, 256), lambda b: (b, 0)))
    out_shape.append(jax.ShapeDtypeStruct((B, LANES, 256), F32))
    out_specs.append(pl.BlockSpec((None, LANES, 256), lambda b: (b, 0, 0)))
    L = g.Lt + g.Lm
    return pl.pallas_call(
        functools.partial(_ssd_body, g),
        grid=(B,),
        in_specs=specs,
        out_specs=out_specs,
        out_shape=out_shape,
        scratch_shapes=[pltpu.VMEM((8 + L, 512), F32), pltpu.VMEM((L, 512), F32)],
        compiler_params=_params(("arbitrary",)),
        name="ssd",
    )(*ins)


def _pad_cols(w, b):
    src = {}
    off = 0
    for name, wd in _IN_SPLITS:
        src[name] = (off, wd)
        off += wd
    D = w.shape[0]
    wp = jnp.zeros((D, NP_COLS), w.dtype)
    bp = jnp.zeros((NP_COLS,), b.dtype)

    def put(wp, bp, name, dst):
        o, wd = src[name]
        return (lax.dynamic_update_slice(wp, w[:, o:o + wd], (0, dst)),
                lax.dynamic_update_slice(bp, b[o:o + wd], (dst,)))
    for name, (dst, _) in _COL.items():
        if name == 'small':
            for sname, lane in _SMALL.items():
                wp, bp = put(wp, bp, sname, dst + lane)
        else:
            wp, bp = put(wp, bp, name, dst)
    return wp.astype(BF16), bp[None, :]


def _rope_tables(pos):
    half = HD // 2
    freqs = ROPE_THETA ** (-jnp.arange(half, dtype=F32) / half)
    ang = pos.astype(F32)[:, None] * freqs[None, :]
    cos = jnp.cos(ang)
    sin = jnp.sin(ang)
    cos4 = jnp.concatenate([cos, cos, cos, cos], axis=1)
    sin4 = jnp.concatenate([-sin, sin, -sin, sin], axis=1)
    return cos4, sin4


def _lane_vec(values, lane0, width=LANES):
    return jnp.zeros((1, width), F32).at[0, lane0:lane0 + values.shape[0]].set(values.astype(F32))


def kernel(x_prompt, x_sample, cache_fox_k, cache_fox_v, cache_fox_logf, cache_dsa_k, cache_dsa_v, cache_dsa_kidx,
           state_gla, state_ssd, state_ssd_conv, page_table, meta, w_in, b_in, norm_mix, gla_w_a2, gla_b_a2,
           gla_norm, ssd_conv_w, ssd_conv_b, ssd_dt_bias, ssd_a_log, ssd_d, ssd_norm, w_br_fox, w_br_gla,
           w_br_dsa, w_br_ssd, w_out, norm_mlp, w_up, w_down, norm_final):
    depth = w_in.shape[0]
    Bp, Lmain, D = x_prompt.shape
    Bs, Ls, _ = x_sample.shape
    n_pages = page_table.shape[1]
    Lpast = n_pages * cache_fox_k.shape[2]
    n_meta = meta.shape[0]
    gp = Geom(B=Bp, Lm=Lmain, Lt=n_meta, Lpast=0, tail_row0=0)
    gs = Geom(B=Bs, Lm=0, Lt=Ls, Lpast=Lpast, tail_row0=Bp * n_meta)
    Tm = Bp * Lmain
    Tt = Bp * n_meta + Bs * Ls

    h_m = x_prompt.reshape(Tm, D)
    h_t = jnp.concatenate([jnp.broadcast_to(meta.astype(F32)[None], (Bp, n_meta, D)).reshape(Bp * n_meta, D),
                           x_sample.reshape(Bs * Ls, D)], axis=0)
    pos_m = jnp.tile(n_meta + jnp.arange(Lmain, dtype=I32), Bp)
    pos_t = jnp.concatenate([jnp.tile(jnp.arange(n_meta, dtype=I32), Bp),
                             jnp.tile(Lpast + jnp.arange(Ls, dtype=I32), Bs)])
    cos_m, sin_m = _rope_tables(pos_m)
    cos_t, sin_t = _rope_tables(pos_t)

    pools_kv = [jnp.transpose(c, (0, 1, 3, 4, 2)) for c in (cache_fox_k, cache_fox_v, cache_dsa_k, cache_dsa_v)]
    pool_kidx = jnp.transpose(cache_dsa_kidx, (0, 1, 3, 2))
    pool_logf = jnp.pad(jnp.transpose(cache_fox_logf, (0, 1, 3, 2)), ((0, 0), (0, 0), (0, 8 - HEADS), (0, 0)))

    tm_m = 512
    tm_t = Tt // 3 if (Tt % 3 == 0 and (Tt // 3) % 8 == 0) else Tt
    tn = 1536
    outs_p = [[] for _ in range(9)]
    outs_s = [[] for _ in range(9)]
    y_m = y_t = None
    for l in range(depth):
        wp, bp = _pad_cols(w_in[l], b_in[l])
        g_mix = norm_mix[l][None, :]
        pr_m = _proj(h_m, g_mix, wp, bp, tm_m, tn)
        pr_t = _proj(h_t, g_mix, wp, bp, tm_t, tn)
        dq_m, dk_m, iq_m, ik_m, lf_m = _prep(pr_m, cos_m, sin_m, tm_m)
        dq_t, dk_t, iq_t, ik_t, lf_t = _prep(pr_t, cos_t, sin_t, tm_t)

        past = _gather_pages(l, page_table, pools_kv + [pool_kidx, pool_logf])
        fkT, fvT, dkT, dvT, ikT, lfT = past

        cm_p, ct_p, crm_p, crt_p = _fox_c(gp, lf_m, lf_t)
        ct_s, crm_s, crt_s = _fox_c(gs, lfT, lf_t)
        ent = lambda name: (pr_m, pr_t, _cblk(name)[0])
        fox_t_p, fox_m = _attention('fox', gp, ent('fox_q'), ent('fox_k'), ent('fox_v'), None,
                                    fox=(cm_p, ct_p, crm_p, crt_p))
        fox_t_s, = _attention('fox', gs, ent('fox_q'), ent('fox_k'), ent('fox_v'), (fkT, fvT),
                              fox=(None, ct_s, crm_s, crt_s))
        dsa_args = ((iq_m, iq_t, 0), (pr_m, pr_t, _cblk('small')[0]), (ik_m, ik_t, 0))
        dsa_t_p, dsa_m = _attention('dsa', gp, (dq_m, dq_t, 0), (dk_m, dk_t, 0), ent('dsa_v'), None,
                                    dsa=dsa_args + (None,))
        dsa_t_s, = _attention('dsa', gs, (dq_m, dq_t, 0), (dk_m, dk_t, 0), ent('dsa_v'), (dkT, dvT),
                              dsa=dsa_args + (ikT,))
        wa2 = jnp.zeros((LANES, LANES), F32).at[_SMALL['gla_a']:_SMALL['gla_a'] + gla_w_a2.shape[1]].set(
            gla_w_a2[l]).astype(BF16)
        ba2 = gla_b_a2[l][None, :]
        gn = jnp.tile(gla_norm[l], HEADS)[None, :]
        eye = jnp.eye(HEADS, dtype=F32)
        s0 = (state_gla[l][:, :, :, None, :] * eye[None, :, None, :, None]).reshape(Bs, LANES, 256)
        gla_t_p, gla_m, gla_S_p = _gla(gp, pr_m, pr_t, None, wa2, ba2, gn)
        gla_t_s, gla_S_s = _gla(gs, pr_m, pr_t, s0, wa2, ba2, gn)
        dtb = _lane_vec(ssd_dt_bias[l], _DT_LANE)
        aneg = _lane_vec(-jnp.exp(ssd_a_log[l].astype(F32)), _DT_LANE)
        dvec = jnp.repeat(ssd_d[l].astype(F32), HD)[None, :]
        snorm = ssd_norm[l][None, :]
        grp = (jnp.arange(2)[:, None] == (jnp.arange(HEADS) // 2)[None, :]).astype(F32)
        hT = jnp.transpose(state_ssd[l], (0, 3, 1, 2))
        h0 = (hT[:, None] * grp[None, :, None, :, None]).reshape(Bs, LANES, 256)
        ssd_args = (ssd_conv_w[l], ssd_conv_b[l][None, :], dtb, aneg, dvec, snorm)
        ssd_t_p, ssd_m, ssd_H_p = _ssd(gp, pr_m, pr_t, None, None, *ssd_args)
        ssd_t_s, ssd_H_s = _ssd(gs, pr_m, pr_t, state_ssd_conv[l], h0, *ssd_args)

        cat = lambda a, b: jnp.concatenate([a, b], axis=0)
        ys_m = (fox_m, gla_m, dsa_m, ssd_m)
        ys_t = (cat(fox_t_p, fox_t_s), cat(gla_t_p, gla_t_s), cat(dsa_t_p, dsa_t_s), cat(ssd_t_p, ssd_t_s))
        wbr = jnp.stack([w_br_fox[l], w_br_gla[l], w_br_dsa[l], w_br_ssd[l]]).astype(BF16)
        wo = w_out[l].astype(BF16)
        h_m = _merge(ys_m, pr_m, h_m, wbr, wo, 256)
        h_t = _merge(ys_t, pr_t, h_t, wbr, wo, tm_t)
        last = l == depth - 1
        g_mlp = norm_mlp[l][None, :]
        wu = w_up[l].astype(BF16)
        wd = w_down[l].astype(BF16)
        gf = norm_final[None, :]
        res_m = _mlp(h_m, g_mlp, wu, wd, gf, last, tm_m)
        res_t = _mlp(h_t, g_mlp, wu, wd, gf, last, tm_t)
        h_m, h_t = res_m[0], res_t[0]
        if last:
            y_m, y_t = res_m[1], res_t[1]

        def seq_p(main, tail, w):
            return jnp.concatenate([tail[:Bp * n_meta].reshape(Bp, n_meta, w), main.reshape(Bp, Lmain, w)], axis=1)

        def seq_s(tail, w):
            return tail[Bp * n_meta:].reshape(Bs, Ls, w)

        def colm(name):
            o, w = _COL[name]
            return pr_m[:, o:o + w], pr_t[:, o:o + w], w
        Lp = n_meta + Lmain
        fk = colm('fox_k'); fv = colm('fox_v'); dv = colm('dsa_v'); xbc = colm('ssd_xbc')

        def gla_state(S, B):
            S4 = S.reshape(B, HEADS, GLA_DK, HEADS, 64)
            return jnp.stack([S4[:, hh, :, hh, :] for hh in range(HEADS)], axis=1)

        def ssd_state(H, B):
            H4 = H.reshape(B, 2, 64, HEADS, 64)
            return jnp.stack([jnp.swapaxes(H4[:, hh // 2, :, hh, :], 1, 2) for hh in range(HEADS)], axis=1)

        xbc_p = seq_p(xbc[0], xbc[1], 512)
        xbc_s = seq_s(xbc[1], 512)
        layer_p = (seq_p(fk[0], fk[1], 256).reshape(Bp, Lp, HEADS, HD),
                   seq_p(fv[0], fv[1], 256).reshape(Bp, Lp, HEADS, HD),
                   seq_p(lf_m, lf_t, LANES)[:, :, :HEADS],
                   seq_p(dk_m, dk_t, 256).reshape(Bp, Lp, HEADS, HD),
                   seq_p(dv[0], dv[1], 256).reshape(Bp, Lp, HEADS, HD),
                   seq_p(ik_m, ik_t, LANES)[:, :, :HD],
                   gla_state(gla_S_p, Bp), ssd_state(ssd_H_p, Bp),
                   xbc_p[:, Lp - (SSD_CONV - 1):])
        conv_s = jnp.concatenate([state_ssd_conv[l].astype(F32), xbc_s], axis=1)[:, Ls:]
        layer_s = (seq_s(fk[1], 256).reshape(Bs, Ls, HEADS, HD),
                   seq_s(fv[1], 256).reshape(Bs, Ls, HEADS, HD),
                   seq_s(lf_t, LANES)[:, :, :HEADS],
                   seq_s(dk_t, 256).reshape(Bs, Ls, HEADS, HD),
                   seq_s(dv[1], 256).reshape(Bs, Ls, HEADS, HD),
                   seq_s(ik_t, LANES)[:, :, :HD],
                   gla_state(gla_S_s, Bs), ssd_state(ssd_H_s, Bs),
                   conv_s)
        for k_ in range(9):
            outs_p[k_].append(layer_p[k_])
            outs_s[k_].append(layer_s[k_])

    y_prompt = y_m.reshape(Bp, Lmain, D)
    y_sample = y_t[Bp * n_meta:].reshape(Bs, Ls, D)
    return (y_prompt, y_sample) + tuple(jnp.stack(c) for c in outs_p) + tuple(jnp.stack(c) for c in outs_s)
```

```python
import functools
from typing import NamedTuple

import numpy as np
import jax
import jax.numpy as jnp
from jax import lax
from jax.experimental import pallas as pl
from jax.experimental.pallas import tpu as pltpu

F32 = jnp.float32
BF16 = jnp.bfloat16
I32 = jnp.int32

D_MODEL = 1024
N_META = 16
PAGE_SIZE = 128
CHUNK = 64
ROPE_THETA = 10000.0
EPS = 1e-6
NEG = -1e30
HEADS = 4
HD = 64
GLA_DK = 32
GLA_TAU = 16.0
DSA_TOPK = 256
SSD_CONV = 4
D_FF = 4 * D_MODEL

LANES = 128
VMEM_LIMIT = 56 * 1024 * 1024

_IN_SPLITS = (('fox_q', 256), ('fox_k', 256), ('fox_v', 256), ('fox_f', 4),
              ('gla_q', 128), ('gla_k', 128), ('gla_v', 256), ('gla_a', 16), ('gla_r', 256),
              ('dsa_q', 256), ('dsa_k', 256), ('dsa_v', 256),
              ('idx_q', 256), ('idx_w', 4), ('idx_k', 64),
              ('ssd_z', 256), ('ssd_xbc', 512), ('ssd_dt', 4),
              ('gates', 4096))
_COL = dict(gates=(0, 4096), ssd_xbc=(4096, 512), fox_q=(4608, 256), fox_k=(4864, 256), fox_v=(5120, 256),
            dsa_q=(5376, 256), dsa_k=(5632, 256), dsa_v=(5888, 256), idx_q=(6144, 256), ssd_z=(6400, 256),
            gla_v=(6656, 256), gla_r=(6912, 256), gla_q=(7168, 128), gla_k=(7296, 128),
            small=(7424, 128), idx_k=(7552, 128))
NP_COLS = 7680
_SMALL = dict(fox_f=0, idx_w=4, ssd_dt=8, gla_a=16)


def _cblk(name):
    off, w = _COL[name]
    assert off % w == 0
    return off // w, w


class Geom(NamedTuple):
    B: int
    Lm: int
    Lt: int
    Lpast: int
    tail_row0: int

    @property
    def km(self):
        return self.Lm if self.Lm else self.Lpast

    @property
    def pos_main0(self):
        return self.Lt if self.Lm else 0

    @property
    def pos_tail0(self):
        return 0 if self.Lm else self.Lpast

    @property
    def tail_blk0(self):
        assert self.tail_row0 % self.Lt == 0
        return self.tail_row0 // self.Lt


def _main_spec(g, w, cblk=0):
    return pl.BlockSpec((g.Lm, w), lambda b, *_: (b, cblk))


def _tail_spec(g, w, cblk=0):
    return pl.BlockSpec((g.Lt, w), lambda b, *_: (g.tail_blk0 + b, cblk))


def _params(sem):
    return pltpu.CompilerParams(dimension_semantics=sem, vmem_limit_bytes=VMEM_LIMIT)


def _log_sigmoid(x):
    return jnp.minimum(x, 0.0) - jnp.log1p(jnp.exp(-jnp.abs(x)))


def _softplus(x):
    return jnp.maximum(x, 0.0) + jnp.log1p(jnp.exp(-jnp.abs(x)))


def _silu(x):
    return x * jax.nn.sigmoid(x)


def _split3(x):
    hi = x.astype(BF16)
    r1 = x - hi.astype(F32)
    mid = r1.astype(BF16)
    lo = (r1 - mid.astype(F32)).astype(BF16)
    return hi, mid, lo


def _tri_left(tri, x):
    hi, mid, lo = _split3(x)
    d = functools.partial(jnp.dot, preferred_element_type=F32)
    return d(tri, hi) + d(tri, mid) + d(tri, lo)


def _tri_right(x, tri):
    hi, mid, lo = _split3(x)
    d = functools.partial(jnp.dot, preferred_element_type=F32)
    return d(hi, tri) + d(mid, tri) + d(lo, tri)


def _lower_tri(n, dtype=BF16):
    r = lax.broadcasted_iota(I32, (n, n), 0)
    c = lax.broadcasted_iota(I32, (n, n), 1)
    return (r >= c).astype(dtype)


def _upper_tri(n, dtype=BF16):
    r = lax.broadcasted_iota(I32, (n, n), 0)
    c = lax.broadcasted_iota(I32, (n, n), 1)
    return (r <= c).astype(dtype)


def _dot(a, b):
    return jnp.dot(a, b, preferred_element_type=F32)


def _dot_nt(a, b):
    return lax.dot_general(a, b, (((1,), (1,)), ((), ())), preferred_element_type=F32)


def _dot_tn(a, b):
    return lax.dot_general(a, b, (((0,), (0,)), ((), ())), preferred_element_type=F32)


def _head_of_lane(shape, width):
    return lax.broadcasted_iota(I32, shape, len(shape) - 1) // width


def _rms(x, g):
    ms = jnp.mean(x * x, axis=-1, keepdims=True)
    return x * lax.rsqrt(ms + EPS) * g


def _proj_body(x_ref, g_ref, w_ref, b_ref, o_ref):
    u = _rms(x_ref[...], g_ref[...]).astype(BF16)
    o_ref[...] = _dot(u, w_ref[...]) + b_ref[...]


def _proj(h, g, w, b, tm, tn):
    T, D = h.shape
    N = w.shape[1]
    return pl.pallas_call(
        _proj_body,
        grid=(N // tn, T // tm),
        in_specs=[pl.BlockSpec((tm, D), lambda j, i: (i, 0)),
                  pl.BlockSpec((1, D), lambda j, i: (0, 0)),
                  pl.BlockSpec((D, tn), lambda j, i: (0, j)),
                  pl.BlockSpec((1, tn), lambda j, i: (0, j))],
        out_specs=pl.BlockSpec((tm, tn), lambda j, i: (i, j)),
        out_shape=jax.ShapeDtypeStruct((T, N), F32),
        compiler_params=_params(("arbitrary", "arbitrary")),
        name="proj",
    )(h, g, w, b)


def _rope128(x, cos, sin_signed):
    lane = lax.broadcasted_iota(I32, x.shape, 1)
    swapped = jnp.where((lane % HD) < HD // 2, pltpu.roll(x, LANES - HD // 2, 1), pltpu.roll(x, HD // 2, 1))
    return x * cos + swapped * sin_signed


def _prep_body(dq_ref, dk_ref, iq_ref, ik_ref, sm_ref, cos_ref, sin_ref, odq, odk, oiq, oik, olf):
    cos = cos_ref[...]
    sin = sin_ref[...]
    for src, dst in ((dq_ref, odq), (dk_ref, odk), (iq_ref, oiq)):
        for half in range(2):
            sl = slice(half * LANES, (half + 1) * LANES)
            dst[:, sl] = _rope128(src[:, sl], cos, sin)
    oik[...] = _rope128(ik_ref[...], cos, sin)
    olf[...] = _log_sigmoid(sm_ref[...])


def _prep(pr, cos, sin, tm):
    T = pr.shape[0]

    def col(name):
        cb, w = _cblk(name)
        return pl.BlockSpec((tm, w), lambda i: (i, cb))
    row = lambda w: pl.BlockSpec((tm, w), lambda i: (i, 0))
    return pl.pallas_call(
        _prep_body,
        grid=(T // tm,),
        in_specs=[col('dsa_q'), col('dsa_k'), col('idx_q'), col('idx_k'), col('small'), row(LANES), row(LANES)],
        out_specs=[row(256), row(256), row(256), row(LANES), row(LANES)],
        out_shape=[jax.ShapeDtypeStruct((T, 256), F32)] * 3 + [jax.ShapeDtypeStruct((T, LANES), F32)] * 2,
        compiler_params=_params(("arbitrary",)),
        name="prep",
    )(pr, pr, pr, pr, pr, cos, sin)


def _merge_body(yf, yg, yd, ys, gates_ref, h_ref, wbr_ref, wout_ref, o_ref):
    merged = None
    for b, y in enumerate((yf, yg, yd, ys)):
        gate = jax.nn.sigmoid(gates_ref[:, b * D_MODEL:(b + 1) * D_MODEL])
        term = gate * _dot(y[...].astype(BF16), wbr_ref[b])
        merged = term if merged is None else merged + term
    o_ref[...] = h_ref[...] + _dot(merged.astype(BF16), wout_ref[...])


def _merge(ys, pr, h, wbr, wout, tm):
    T = h.shape[0]
    row = lambda w: pl.BlockSpec((tm, w), lambda i: (i, 0))
    return pl.pallas_call(
        _merge_body,
        grid=(T // tm,),
        in_specs=[row(256)] * 4 + [row(4 * D_MODEL), row(D_MODEL),
                                   pl.BlockSpec((4, 256, D_MODEL), lambda i: (0, 0, 0)),
                                   pl.BlockSpec((D_MODEL, D_MODEL), lambda i: (0, 0))],
        out_specs=row(D_MODEL),
        out_shape=jax.ShapeDtypeStruct((T, D_MODEL), F32),
        compiler_params=_params(("arbitrary",)),
        name="merge",
    )(*ys, pr, h, wbr, wout)


def _mlp_body(with_final, h_ref, g_ref, wup_ref, wdn_ref, gf_ref, o_ref, *maybe_y):
    h = h_ref[...]
    m = _rms(h, g_ref[...]).astype(BF16)
    acc = h
    for c in range(D_FF // D_MODEL):
        sl = slice(c * D_MODEL, (c + 1) * D_MODEL)
        hid = jnp.square(jnp.maximum(_dot(m, wup_ref[:, sl]), 0.0))
        acc = acc + _dot(hid.astype(BF16), wdn_ref[sl, :])
    o_ref[...] = acc
    if with_final:
        maybe_y[0][...] = _rms(acc, gf_ref[...])


def _mlp(h, g, wup, wdn, gf, with_final, tm):
    T = h.shape[0]
    row = pl.BlockSpec((tm, D_MODEL), lambda i: (i, 0))
    vec = pl.BlockSpec((1, D_MODEL), lambda i: (0, 0))
    n_out = 2 if with_final else 1
    outs = pl.pallas_call(
        functools.partial(_mlp_body, with_final),
        grid=(T // tm,),
        in_specs=[row, vec,
                  pl.BlockSpec((D_MODEL, D_FF), lambda i: (0, 0), pipeline_mode=pl.Buffered(1)),
                  pl.BlockSpec((D_FF, D_MODEL), lambda i: (0, 0), pipeline_mode=pl.Buffered(1)),
                  vec],
        out_specs=[row] * n_out,
        out_shape=[jax.ShapeDtypeStruct((T, D_MODEL), F32)] * n_out,
        compiler_params=_params(("arbitrary",)),
        name="mlp",
    )(h, g, wup, wdn, gf)
    return outs


def _gather_body(layer, n_seq, n_pages, n_t, pt_ref, *refs):
    pools = refs[:n_t]
    outs = refs[n_t:2 * n_t]
    sem = refs[2 * n_t]

    def copies(b):
        slot = b % 2
        for t in range(n_t):
            for p in range(n_pages):
                src = pools[t].at[layer, pt_ref[b, p]]
                yield pltpu.make_async_copy(src, outs[t].at[b, p], sem.at[slot, t, p])

    def start(b):
        for c in copies(b):
            c.start()

    def wait(b):
        for c in copies(b):
            c.wait()

    start(0)

    def step(b, carry):
        start(b)
        wait(b - 1)
        return carry
    lax.fori_loop(1, n_seq, step, 0)
    wait(n_seq - 1)


def _gather_pages(layer, page_table, pools):
    n_seq, n_pages = page_table.shape
    n_t = len(pools)
    out_shape = [jax.ShapeDtypeStruct((n_seq, n_pages) + p.shape[2:], p.dtype) for p in pools]
    any_spec = pl.BlockSpec(memory_space=pl.ANY)
    return pl.pallas_call(
        functools.partial(_gather_body, layer, n_seq, n_pages, n_t),
        in_specs=[pl.BlockSpec(memory_space=pltpu.SMEM)] + [any_spec] * n_t,
        out_specs=[any_spec] * n_t,
        out_shape=out_shape,
        scratch_shapes=[pltpu.SemaphoreType.DMA((2, n_t, n_pages))],
        name="gather_pages",
    )(page_table, *pools)


def _foxc_body(g, *refs):
    if g.Lm:
        lfm, lft, ocm, oct_, orm, ort, pad = refs
    else:
        lfm, lft, oct_, orm, ort, pad = refs
    nblk = g.km // LANES
    tl = _lower_tri(LANES)
    ort[...] = jnp.zeros_like(ort)
    pad[...] = jnp.zeros_like(pad)
    if g.Lm:
        ct = _tri_left(_lower_tri(g.Lt), lft[...])
        oct_[...] = ct
        pad[0:g.Lt, :] = ct
        ort[...] = pad[...].T[0:8, :]
        carry = ct[g.Lt - 1:g.Lt, :]
        for j in range(nblk):
            sl = slice(j * LANES, (j + 1) * LANES)
            c = _tri_left(tl, lfm[sl, :]) + carry
            ocm[sl, :] = c
            orm[:, sl] = c.T[0:8, :]
            carry = c[LANES - 1:LANES, :]
    else:
        tu = _upper_tri(LANES)
        carry = jnp.zeros((8, 1), F32)
        c = None
        for j in range(nblk):
            sl = slice(j * LANES, (j + 1) * LANES)
            c = _tri_right(lfm[j], tu) + carry
            orm[:, sl] = c
            carry = c[:, LANES - 1:LANES]
        pad[0:8, :] = c
        carry_row = pad[...].T[LANES - 1:LANES, :]
        ct = _tri_left(_lower_tri(g.Lt), lft[...]) + carry_row
        oct_[...] = ct
        pad[0:8, :] = jnp.zeros((8, LANES), F32)
        pad[0:g.Lt, :] = ct
        ort[...] = pad[...].T[0:8, :]


def _fox_c(g, lf_main, lf_tail):
    B = g.B
    outs_shape, outs_spec = [], []
    if g.Lm:
        in_specs = [_main_spec(g, LANES), _tail_spec(g, LANES)]
        outs_shape.append(jax.ShapeDtypeStruct((B * g.Lm, LANES), F32))
        outs_spec.append(pl.BlockSpec((g.Lm, LANES), lambda b: (b, 0)))
    else:
        in_specs = [pl.BlockSpec((None, g.km // PAGE_SIZE, 8, PAGE_SIZE), lambda b: (b, 0, 0, 0)),
                    _tail_spec(g, LANES)]
    outs_shape += [jax.ShapeDtypeStruct((B * g.Lt, LANES), F32),
                   jax.ShapeDtypeStruct((B, 8, g.km), F32),
                   jax.ShapeDtypeStruct((B, 8, LANES), F32)]
    outs_spec += [pl.BlockSpec((g.Lt, LANES), lambda b: (b, 0)),
                  pl.BlockSpec((None, 8, g.km), lambda b: (b, 0, 0)),
                  pl.BlockSpec((None, 8, LANES), lambda b: (b, 0, 0))]
    return pl.pallas_call(
        functools.partial(_foxc_body, g),
        grid=(B,),
        in_specs=in_specs,
        out_specs=outs_spec,
        out_shape=outs_shape,
        scratch_shapes=[pltpu.VMEM((LANES, LANES), F32)],
        compiler_params=_params(("arbitrary",)),
        name="fox_c",
    )(lf_main, lf_tail)


def _sort_key(x):
    i = lax.bitcast_convert_type(x + 0.0, I32)
    return i ^ ((i >> 31) & jnp.int32(0x7FFFFFFF))


def _count(mask_m, mask_t):
    return (jnp.sum(mask_m.astype(I32), axis=-1, keepdims=True)
            + jnp.sum(mask_t.astype(I32), axis=-1, keepdims=True))


def _topk_mask(sc_m, sc_t, kpos_m, kpos_t, topk, max_pos):
    key_m = _sort_key(sc_m)
    key_t = _sort_key(sc_t)
    int_min = jnp.int32(-2 ** 31)

    n0 = _count(key_m >= 0, key_t >= 0)
    t0 = jnp.where(n0 >= topk, jnp.int32(0), int_min)

    def t_step(i, t):
        cand = t + lax.shift_left(jnp.int32(1), 30 - i)
        n = _count(key_m >= cand, key_t >= cand)
        return jnp.where(n >= topk, cand, t)
    t = lax.fori_loop(0, 31, t_step, t0)

    gt_m, gt_t = key_m > t, key_t > t
    tie_m, tie_t = key_m == t, key_t == t
    n_gt = _count(gt_m, gt_t)
    need = topk - n_gt
    surplus = _count(tie_m, tie_t) - need
    nbits = int(max_pos).bit_length()

    def tie_cutoff():
        def p_step(i, p):
            cand = p + lax.shift_left(jnp.int32(1), nbits - 1 - i)
            n = _count(tie_m & (kpos_m < cand), tie_t & (kpos_t < cand))
            return jnp.where(n < need, cand, p)
        return lax.fori_loop(0, nbits, p_step, jnp.zeros_like(need))

    def take_all():
        return jnp.full(need.shape, 2 ** nbits - 1, I32)
    p = lax.cond(jnp.max(surplus) > 0, tie_cutoff, take_all)
    return gt_m | (tie_m & (kpos_m <= p)), gt_t | (tie_t & (kpos_t <= p))


def _attn_body(kind, g, topk, *refs):
    refs = list(refs)
    take = lambda n: [refs.pop(0) for _ in range(n)]
    has_q_main = bool(g.Lm)
    R = {}
    R['qt'], = take(1)
    if has_q_main:
        R['qm'], = take(1)
    R['kt'], R['vt'], R['km'], R['vm'] = take(4)
    if kind == 'fox':
        R['cqt'], = take(1)
        if has_q_main:
            R['cqm'], = take(1)
        R['crm'], R['crt'] = take(2)
    else:
        R['iqt'], R['wt'] = take(2)
        if has_q_main:
            R['iqm'], R['wm'] = take(2)
        R['ikt'], R['ikm'] = take(2)
    R['ot'], = take(1)
    if has_q_main:
        R['om'], = take(1)
    R['KT'], R['VT'], R['ktp'], R['vtp'] = take(4)
    if kind == 'dsa':
        R['IKT'], R['iktp'] = take(2)
    assert not refs

    KM, Lt = g.km, g.Lt
    i = pl.program_id(1)

    @pl.when(i == 0)
    def _init():
        if has_q_main:
            for j in range(KM // 256):
                sl = slice(j * 256, (j + 1) * 256)
                kT = R['km'][sl, :].T.astype(BF16)
                vT = R['vm'][sl, :].T.astype(BF16)
                for h in range(HEADS):
                    R['KT'][h, :, sl] = kT[h * HD:(h + 1) * HD, :]
                    R['VT'][h, :, sl] = vT[h * HD:(h + 1) * HD, :]
            if kind == 'dsa':
                for j in range(KM // LANES):
                    sl = slice(j * LANES, (j + 1) * LANES)
                    R['IKT'][:, sl] = R['ikm'][sl, :].T[0:HD, :].astype(BF16)
        else:
            for p in range(KM // PAGE_SIZE):
                sl = slice(p * PAGE_SIZE, (p + 1) * PAGE_SIZE)
                for h in range(HEADS):
                    R['KT'][h, :, sl] = R['km'][p, h].astype(BF16)
                    R['VT'][h, :, sl] = R['vm'][p, h].astype(BF16)
                if kind == 'dsa':
                    R['IKT'][:, sl] = R['ikm'][p].astype(BF16)
        R['ktp'][...] = jnp.zeros_like(R['ktp'])
        R['vtp'][...] = jnp.zeros_like(R['vtp'])
        R['ktp'][0:Lt, :] = R['kt'][...].astype(BF16)
        R['vtp'][0:Lt, :] = R['vt'][...].astype(BF16)
        if kind == 'dsa':
            R['iktp'][...] = jnp.zeros_like(R['iktp'])
            R['iktp'][0:Lt, :] = R['ikt'][...].astype(BF16)

    def process(rows, q, extra, qpos0, out_ref, out_rows, kw):
        qpos = qpos0 + lax.broadcasted_iota(I32, (rows, 1), 0)
        lane_t = lax.broadcasted_iota(I32, (1, LANES), 1)
        kpos_t = g.pos_tail0 + lane_t
        mask_t = (kpos_t <= qpos) & (lane_t < Lt)
        if kw:
            kpos_m = g.pos_main0 + lax.broadcasted_iota(I32, (1, kw), 1)
            mask_m = kpos_m <= qpos
        if kind == 'dsa':
            iq, w = extra
        else:
            cq, = extra
        if kind == 'dsa' and kw + Lt > topk:
            assert kw
            sc_m = jnp.zeros((rows, kw), F32)
            sc_t = jnp.zeros((rows, LANES), F32)
            for h in range(HEADS):
                iqh = iq[:, h * HD:(h + 1) * HD].astype(BF16)
                wh = w[:, _SMALL['idx_w'] + h:_SMALL['idx_w'] + h + 1]
                sc_m = sc_m + wh * jnp.maximum(_dot(iqh, R['IKT'][:, 0:kw]), 0.0)
                sc_t = sc_t + wh * jnp.maximum(_dot_nt(iqh, R['iktp'][:, 0:HD]), 0.0)
            sc_m = jnp.where(mask_m, sc_m, NEG)
            sc_t = jnp.where(lane_t < Lt, jnp.where(mask_t, sc_t, NEG), -jnp.inf)
            sel_m, sel_t = _topk_mask(sc_m, sc_t, kpos_m, kpos_t, topk, g.km + g.Lt)
            mask_m = mask_m & sel_m
            mask_t = mask_t & sel_t
        for h in range(HEADS):
            hs = slice(h * HD, (h + 1) * HD)
            qh = q[:, hs].astype(BF16)
            s_t = _dot_nt(qh, R['ktp'][:, hs]) * (HD ** -0.5)
            if kind == 'fox':
                cqh = cq[:, h:h + 1]
                s_t = s_t + cqh - R['crt'][h:h + 1, :]
            s_t = jnp.where(mask_t, s_t, NEG)
            mx = jnp.max(s_t, axis=-1, keepdims=True)
            if kw:
                s_m = _dot(qh, R['KT'][h, :, 0:kw]) * (HD ** -0.5)
                if kind == 'fox':
                    s_m = s_m + cqh - R['crm'][h:h + 1, 0:kw]
                s_m = jnp.where(mask_m, s_m, NEG)
                mx = jnp.maximum(mx, jnp.max(s_m, axis=-1, keepdims=True))
            p_t = jnp.exp(s_t - mx)
            den = jnp.sum(p_t, axis=-1, keepdims=True)
            o = _dot(p_t.astype(BF16), R['vtp'][:, hs])
            if kw:
                p_m = jnp.exp(s_m - mx)
                den = den + jnp.sum(p_m, axis=-1, keepdims=True)
                o = o + _dot_nt(p_m.astype(BF16), R['VT'][h, :, 0:kw])
            out_ref[out_rows, hs] = o / den

    def extras(suffix, rows_sl):
        if kind == 'fox':
            return (R['cq' + suffix][rows_sl, :],)
        return (R['iq' + suffix][rows_sl, :], R['w' + suffix][rows_sl, :])

    @pl.when(i == 0)
    def _tail_queries():
        all_rows = slice(None)
        process(Lt, R['qt'][...], extras('t', all_rows), g.pos_tail0, R['ot'], all_rows, 0 if has_q_main else KM)

    if has_q_main:
        nqb = g.Lm // LANES
        per = 4 if (nqb % 4 == 0 and nqb > 4) else nqb
        for v in range(nqb // per):
            @pl.when((i > v * per) & (i <= (v + 1) * per))
            def _main_queries(kw=(v + 1) * per * LANES):
                r0 = pl.multiple_of((i - 1) * LANES, LANES)
                rows_sl = pl.ds(r0, LANES)
                process(LANES, R['qm'][rows_sl, :], extras('m', rows_sl), g.pos_main0 + r0, R['om'], rows_sl, kw)


def _attention(kind, g, q, k, v, kv_main, fox=None, dsa=None):
    B, KM, Lt = g.B, g.km, g.Lt
    has_q_main = bool(g.Lm)
    nq = 1 + (g.Lm // LANES if has_q_main else 0)
    ins, specs = [], []

    def add(arr, spec):
        ins.append(arr)
        specs.append(spec)

    def add_tok(entry, w, main=True, tail=True):
        arr_m, arr_t, cb = entry
        if tail:
            add(arr_t, _tail_spec(g, w, cb))
        if main and has_q_main:
            add(arr_m, _main_spec(g, w, cb))

    add_tok(q, 256)
    add_tok(k, 256, main=False)
    add_tok(v, 256, main=False)
    if has_q_main:
        add(k[0], _main_spec(g, 256, k[2]))
        add(v[0], _main_spec(g, 256, v[2]))
    else:
        for a in kv_main:
            add(a, pl.BlockSpec((None, KM // PAGE_SIZE, HEADS, HD, PAGE_SIZE), lambda b, i: (b, 0, 0, 0, 0)))
    if kind == 'fox':
        c_main_col, c_tail_col, c_main_row, c_tail_row = fox
        add(c_tail_col, pl.BlockSpec((Lt, LANES), lambda b, i: (b, 0)))
        if has_q_main:
            add(c_main_col, pl.BlockSpec((g.Lm, LANES), lambda b, i: (b, 0)))
        add(c_main_row, pl.BlockSpec((None, 8, KM), lambda b, i: (b, 0, 0)))
        add(c_tail_row, pl.BlockSpec((None, 8, LANES), lambda b, i: (b, 0, 0)))
    else:
        iq, w, ik, ikT_past = dsa
        add(iq[1], _tail_spec(g, 256, iq[2]))
        add(w[1], _tail_spec(g, LANES, w[2]))
        if has_q_main:
            add(iq[0], _main_spec(g, 256, iq[2]))
            add(w[0], _main_spec(g, LANES, w[2]))
        add(ik[1], _tail_spec(g, LANES, ik[2]))
        if has_q_main:
            add(ik[0], _main_spec(g, LANES, ik[2]))
        else:
            add(ikT_past, pl.BlockSpec((None, KM // PAGE_SIZE, HD, PAGE_SIZE), lambda b, i: (b, 0, 0, 0)))

    out_shape = [jax.ShapeDtypeStruct((B * Lt, 256), F32)]
    out_specs = [pl.BlockSpec((Lt, 256), lambda b, i: (b, 0))]
    if has_q_main:
        out_shape.append(jax.ShapeDtypeStruct((B * g.Lm, 256), F32))
        out_specs.append(pl.BlockSpec((g.Lm, 256), lambda b, i: (b, 0)))
    scratch = [pltpu.VMEM((HEADS, HD, KM), BF16), pltpu.VMEM((HEADS, HD, KM), BF16),
               pltpu.VMEM((LANES, 256), BF16), pltpu.VMEM((LANES, 256), BF16)]
    if kind == 'dsa':
        scratch += [pltpu.VMEM((HD, KM), BF16), pltpu.VMEM((LANES, LANES), BF16)]
    topk = min(DSA_TOPK, (KM + Lt) // 4)
    outs = pl.pallas_call(
        functools.partial(_attn_body, kind, g, topk),
        grid=(B, nq),
        in_specs=specs,
        out_specs=out_specs,
        out_shape=out_shape,
        scratch_shapes=scratch,
        compiler_params=_params(("arbitrary", "arbitrary")),
        name=kind + "_attn",
    )(*ins)
    return outs


def _gla_chunk(c, q, k, v, r, small, S, wa2, ba2, gnorm):
    logit = _dot(small.astype(BF16), wa2) + ba2
    gate = _log_sigmoid(logit) / GLA_TAU
    b = _tri_left(_lower_tri(c), gate)
    bT = b.T
    kT = k.T
    qs = q * (GLA_DK ** -0.5)
    row = lax.broadcasted_iota(I32, (c, c), 0)
    col = lax.broadcasted_iota(I32, (c, c), 1)
    causal = row >= col
    head256 = _head_of_lane((c, 256), 64)
    vb = v.astype(BF16)
    y = _dot((qs * jnp.exp(b)).astype(BF16), S.astype(BF16))
    for h in range(HEADS):
        att = jnp.zeros((c, c), F32)
        for d in range(GLA_DK):
            j = h * GLA_DK + d
            decay = jnp.exp(jnp.minimum(b[:, j:j + 1] - bT[j:j + 1, :], 0.0))
            att = att + (qs[:, j:j + 1] * kT[j:j + 1, :]) * decay
        att = jnp.where(causal, att, 0.0)
        y = y + jnp.where(head256 == h, _dot(att.astype(BF16), vb), 0.0)
    b_last_row = b[c - 1:c, :]
    b_last_col = bT[:, c - 1:c]
    kd = (k * jnp.exp(b_last_row - b)).astype(BF16)
    upd = _dot_tn(kd, vb)
    diag = (lax.broadcasted_iota(I32, (LANES, 256), 0) // GLA_DK) == _head_of_lane((LANES, 256), 64)
    S_new = jnp.exp(b_last_col) * S + jnp.where(diag, upd, 0.0)
    inv = jnp.zeros((c, 256), F32)
    for h in range(HEADS):
        ms = jnp.sum(jnp.where(head256 == h, y * y, 0.0), axis=-1, keepdims=True) / 64.0
        inv = jnp.where(head256 == h, lax.rsqrt(ms + EPS), inv)
    out = (y * inv * gnorm) * _silu(r)
    return out, S_new


def _gla_body(g, *refs):
    refs = list(refs)
    take = lambda n: [refs.pop(0) for _ in range(n)]
    qt, kt, vt, rt, st = take(5)
    if g.Lm:
        qm, km, vm, rm, sm = take(5)
    else:
        s0, = take(1)
    wa2, ba2, gnorm = take(3)
    yt, = take(1)
    if g.Lm:
        ym, = take(1)
    s_out, = take(1)
    assert not refs
    w = wa2[...]
    bb = ba2[...]
    gn = gnorm[...]
    S = jnp.zeros((LANES, 256), F32) if g.Lm else s0[...]
    out, S = _gla_chunk(g.Lt, qt[...], kt[...], vt[...], rt[...], st[...], S, w, bb, gn)
    yt[...] = out
    if g.Lm:
        s_out[...] = S

        def step(ci, carry):
            sl = pl.ds(pl.multiple_of(ci * CHUNK, CHUNK), CHUNK)
            o, s_new = _gla_chunk(CHUNK, qm[sl, :], km[sl, :], vm[sl, :], rm[sl, :], sm[sl, :], s_out[...], w, bb, gn)
            ym[sl, :] = o
            s_out[...] = s_new
            return carry
        lax.fori_loop(0, g.Lm // CHUNK, step, 0)
    else:
        s_out[...] = S


def _gla(g, pr_m, pr_t, s0, wa2, ba2, gnorm):
    B = g.B
    ins, specs = [], []
    for name in ('gla_q', 'gla_k', 'gla_v', 'gla_r', 'small'):
        cb, w = _cblk(name)
        ins.append(pr_t)
        specs.append(_tail_spec(g, w, cb))
    if g.Lm:
        for name in ('gla_q', 'gla_k', 'gla_v', 'gla_r', 'small'):
            cb, w = _cblk(name)
            ins.append(pr_m)
            specs.append(_main_spec(g, w, cb))
    else:
        ins.append(s0)
        specs.append(pl.BlockSpec((None, LANES, 256), lambda b: (b, 0, 0)))
    ins += [wa2, ba2, gnorm]
    specs += [pl.BlockSpec((LANES, LANES), lambda b: (0, 0)), pl.BlockSpec((1, LANES), lambda b: (0, 0)),
              pl.BlockSpec((1, 256), lambda b: (0, 0))]
    out_shape = [jax.ShapeDtypeStruct((B * g.Lt, 256), F32)]
    out_specs = [pl.BlockSpec((g.Lt, 256), lambda b: (b, 0))]
    if g.Lm:
        out_shape.append(jax.ShapeDtypeStruct((B * g.Lm, 256), F32))
        out_specs.append(pl.BlockSpec((g.Lm, 256), lambda b: (b, 0)))
    out_shape.append(jax.ShapeDtypeStruct((B, LANES, 256), F32))
    out_specs.append(pl.BlockSpec((None, LANES, 256), lambda b: (b, 0, 0)))
    return pl.pallas_call(
        functools.partial(_gla_body, g),
        grid=(B,),
        in_specs=specs,
        out_specs=out_specs,
        out_shape=out_shape,
        compiler_params=_params(("arbitrary",)),
        name="gla",
    )(*ins)


_DT_LANE = _SMALL['ssd_dt']


def _expand_heads(x, c):
    head256 = _head_of_lane((c, 256), 64)
    out = jnp.zeros((c, 256), F32)
    for h in range(HEADS):
        out = jnp.where(head256 == h, x[:, _DT_LANE + h:_DT_LANE + h + 1], out)
    return out


def _ssd_chunk(c, xbc, z, small, H, dtb, aneg, dvec, norm):
    x = xbc[:, 0:256]
    Bm = xbc[:, 256:384]
    Cm = xbc[:, 384:512]
    dt = _softplus(small + dtb)
    a = dt * aneg
    cum = _tri_left(_lower_tri(c), a)
    cumT = cum.T
    dtT = dt.T
    row = lax.broadcasted_iota(I32, (c, c), 0)
    col = lax.broadcasted_iota(I32, (c, c), 1)
    causal = row >= col
    head256 = _head_of_lane((c, 256), 64)
    xb = x.astype(BF16)
    Cb = Cm.astype(BF16)
    Bb = Bm.astype(BF16)
    y = _dot(Cb, H.astype(BF16)) * _expand_heads(jnp.exp(cum), c)
    cb = [_dot_nt(Cb[:, gi * 64:(gi + 1) * 64], Bb[:, gi * 64:(gi + 1) * 64]) for gi in range(2)]
    for h in range(HEADS):
        j = _DT_LANE + h
        seg = jnp.exp(jnp.minimum(cum[:, j:j + 1] - cumT[j:j + 1, :], 0.0))
        m = jnp.where(causal, cb[h // 2] * seg, 0.0) * dtT[j:j + 1, :]
        y = y + jnp.where(head256 == h, _dot(m.astype(BF16), xb), 0.0)
    last_row = cum[c - 1:c, :]
    wgt = jnp.exp(last_row - cum) * dt
    xw = (x * _expand_heads(wgt, c)).astype(BF16)
    upd = _dot_tn(Bb, xw)
    same_group = (lax.broadcasted_iota(I32, (LANES, 256), 0) // 64) == (_head_of_lane((LANES, 256), 64) // 2)
    decay = _expand_heads(jnp.exp(last_row), 1)
    H_new = decay * H + jnp.where(same_group, upd, 0.0)
    sy = y + dvec * x
    gated = sy * _silu(z)
    out = _rms(gated, norm)
    return out, H_new


def _ssd_body(g, *refs):
    refs = list(refs)
    take = lambda n: [refs.pop(0) for _ in range(n)]
    xt, zt, st = take(3)
    if g.Lm:
        xm, zm, sm = take(3)
    else:
        buf0, h0 = take(2)
    cw, cb, dtb, aneg, dvec, norm = take(6)
    yt, = take(1)
    if g.Lm:
        ym, = take(1)
    h_out, = take(1)
    xin, xc = take(2)
    assert not refs
    Lt, Lm = g.Lt, g.Lm
    L = Lt + Lm
    if g.Lm:
        xin[0:8, :] = jnp.zeros((8, 512), F32)
    else:
        xin[0:8, :] = jnp.zeros((8, 512), F32)
        xin[5:8, :] = buf0[...]
    xin[8:8 + Lt, :] = xt[...]
    if g.Lm:
        xin[8 + Lt:8 + L, :] = xm[...]
    w = cw[...]
    bias = cb[...]

    def conv(blk):
        n = blk.shape[0] - 8
        acc = bias + blk[5:5 + n] * w[0:1, :]
        acc = acc + blk[6:6 + n] * w[1:2, :]
        acc = acc + blk[7:7 + n] * w[2:3, :]
        acc = acc + blk[8:8 + n] * w[3:4, :]
        return _silu(acc)

    xc[0:Lt, :] = conv(xin[0:8 + Lt, :])
    if g.Lm:
        def cstep(ci, carry):
            r0 = pl.multiple_of(Lt + ci * LANES, 8)
            xc[pl.ds(r0, LANES), :] = conv(xin[pl.ds(r0, LANES + 8), :])
            return carry
        lax.fori_loop(0, Lm // LANES, cstep, 0)

    consts = (dtb[...], aneg[...], dvec[...], norm[...])
    H = jnp.zeros((LANES, 256), F32) if g.Lm else h0[...]
    out, H = _ssd_chunk(Lt, xc[0:Lt, :], zt[...], st[...], H, *consts)
    yt[...] = out
    h_out[...] = H
    if g.Lm:
        def step(ci, carry):
            sl = pl.ds(pl.multiple_of(ci * CHUNK, CHUNK), CHUNK)
            slc = pl.ds(pl.multiple_of(Lt + ci * CHUNK, 8), CHUNK)
            o, h_new = _ssd_chunk(CHUNK, xc[slc, :], zm[sl, :], sm[sl, :], h_out[...], *consts)
            ym[sl, :] = o
            h_out[...] = h_new
            return carry
        lax.fori_loop(0, Lm // CHUNK, step, 0)


def _ssd(g, pr_m, pr_t, buf0, h0, cw, cb, dtb, aneg, dvec, norm):
    B = g.B
    ins, specs = [], []
    names = ('ssd_xbc', 'ssd_z', 'small')
    for name in names:
        cbk, w = _cblk(name)
        ins.append(pr_t)
        specs.append(_tail_spec(g, w, cbk))
    if g.Lm:
        for name in names:
            cbk, w = _cblk(name)
            ins.append(pr_m)
            specs.append(_main_spec(g, w, cbk))
    else:
        ins += [buf0, h0]
        specs += [pl.BlockSpec((None, SSD_CONV - 1, 512), lambda b: (b, 0, 0)),
                  pl.BlockSpec((None, LANES, 256), lambda b: (b, 0, 0))]
    ins += [cw, cb, dtb, aneg, dvec, norm]
    specs += [pl.BlockSpec((SSD_CONV, 512), lambda b: (0, 0)), pl.BlockSpec((1, 512), lambda b: (0, 0)),
              pl.BlockSpec((1, LANES), lambda b: (0, 0)), pl.BlockSpec((1, LANES), lambda b: (0, 0)),
              pl.BlockSpec((1, 256), lambda b: (0, 0)), pl.BlockSpec((1, 256), lambda b: (0, 0))]
    out_shape = [jax.ShapeDtypeStruct((B * g.Lt, 256), F32)]
    out_specs = [pl.BlockSpec((g.Lt, 256), lambda b: (b, 0))]
    if g.Lm:
        out_shape.append(jax.ShapeDtypeStruct((B * g.Lm, 256), F32))
        out_specs.append(pl.BlockSpec((g.Lm, 256), lambda b: (b, 0)))
    out_shape.append(jax.ShapeDtypeStruct((B, LANES, 256), F32))
    out_specs.append(pl.BlockSpec((None, LANES, 256), lambda b: (b, 0, 0)))
    L = g.Lt + g.Lm
    return pl.pallas_call(
        functools.partial(_ssd_body, g),
        grid=(B,),
        in_specs=specs,
        out_specs=out_specs,
        out_shape=out_shape,
        scratch_shapes=[pltpu.VMEM((8 + L, 512), F32), pltpu.VMEM((L, 512), F32)],
        compiler_params=_params(("arbitrary",)),
        name="ssd",
    )(*ins)


def _pad_cols(w, b):
    src = {}
    off = 0
    for name, wd in _IN_SPLITS:
        src[name] = (off, wd)
        off += wd
    wb = jnp.concatenate([w, b[None, :]], axis=0)
    pieces = []
    pos = 0

    def emit(name, dst):
        nonlocal pos
        if dst > pos:
            pieces.append(jnp.zeros((wb.shape[0], dst - pos), wb.dtype))
        o, wd = src[name]
        pieces.append(wb[:, o:o + wd])
        pos = dst + wd
    for name, (dst, _) in sorted(_COL.items(), key=lambda kv: kv[1][0]):
        if name == 'small':
            for sname, lane in sorted(_SMALL.items(), key=lambda kv: kv[1]):
                emit(sname, dst + lane)
        else:
            emit(name, dst)
    if pos < NP_COLS:
        pieces.append(jnp.zeros((wb.shape[0], NP_COLS - pos), wb.dtype))
    wbp = jnp.concatenate(pieces, axis=1)
    return wbp[:-1].astype(BF16), wbp[-1:]


def _rope_tables(pos):
    half = HD // 2
    freqs = ROPE_THETA ** (-jnp.arange(half, dtype=F32) / half)
    ang = pos.astype(F32)[:, None] * freqs[None, :]
    cos = jnp.cos(ang)
    sin = jnp.sin(ang)
    cos4 = jnp.concatenate([cos, cos, cos, cos], axis=1)
    sin4 = jnp.concatenate([-sin, sin, -sin, sin], axis=1)
    return cos4, sin4


def _lane_vec(values, lane0, width=LANES):
    return jnp.zeros((1, width), F32).at[0, lane0:lane0 + values.shape[0]].set(values.astype(F32))


def kernel(x_prompt, x_sample, cache_fox_k, cache_fox_v, cache_fox_logf, cache_dsa_k, cache_dsa_v, cache_dsa_kidx,
           state_gla, state_ssd, state_ssd_conv, page_table, meta, w_in, b_in, norm_mix, gla_w_a2, gla_b_a2,
           gla_norm, ssd_conv_w, ssd_conv_b, ssd_dt_bias, ssd_a_log, ssd_d, ssd_norm, w_br_fox, w_br_gla,
           w_br_dsa, w_br_ssd, w_out, norm_mlp, w_up, w_down, norm_final):
    depth = w_in.shape[0]
    Bp, Lmain, D = x_prompt.shape
    Bs, Ls, _ = x_sample.shape
    n_pages = page_table.shape[1]
    Lpast = n_pages * cache_fox_k.shape[2]
    n_meta = meta.shape[0]
    gp = Geom(B=Bp, Lm=Lmain, Lt=n_meta, Lpast=0, tail_row0=0)
    gs = Geom(B=Bs, Lm=0, Lt=Ls, Lpast=Lpast, tail_row0=Bp * n_meta)
    Tm = Bp * Lmain
    Tt = Bp * n_meta + Bs * Ls

    h_m = x_prompt.reshape(Tm, D)
    h_t = jnp.concatenate([jnp.broadcast_to(meta.astype(F32)[None], (Bp, n_meta, D)).reshape(Bp * n_meta, D),
                           x_sample.reshape(Bs * Ls, D)], axis=0)
    pos_m = jnp.tile(n_meta + jnp.arange(Lmain, dtype=I32), Bp)
    pos_t = jnp.concatenate([jnp.tile(jnp.arange(n_meta, dtype=I32), Bp),
                             jnp.tile(Lpast + jnp.arange(Ls, dtype=I32), Bs)])
    cos_m, sin_m = _rope_tables(pos_m)
    cos_t, sin_t = _rope_tables(pos_t)

    pools_kv = [jnp.transpose(c, (0, 1, 3, 4, 2)) for c in (cache_fox_k, cache_fox_v, cache_dsa_k, cache_dsa_v)]
    pool_kidx = jnp.transpose(cache_dsa_kidx, (0, 1, 3, 2))
    pool_logf = jnp.pad(jnp.transpose(cache_fox_logf, (0, 1, 3, 2)), ((0, 0), (0, 0), (0, 8 - HEADS), (0, 0)))

    tm_m = 512
    tm_t = Tt // 3 if (Tt % 3 == 0 and (Tt // 3) % 8 == 0) else Tt
    tn = 1536
    outs_p = [[] for _ in range(9)]
    outs_s = [[] for _ in range(9)]
    y_m = y_t = None
    for l in range(depth):
        wp, bp = _pad_cols(w_in[l], b_in[l])
        g_mix = norm_mix[l][None, :]
        pr_m = _proj(h_m, g_mix, wp, bp, tm_m, tn)
        pr_t = _proj(h_t, g_mix, wp, bp, tm_t, tn)
        dq_m, dk_m, iq_m, ik_m, lf_m = _prep(pr_m, cos_m, sin_m, tm_m)
        dq_t, dk_t, iq_t, ik_t, lf_t = _prep(pr_t, cos_t, sin_t, tm_t)

        past = _gather_pages(l, page_table, pools_kv + [pool_kidx, pool_logf])
        fkT, fvT, dkT, dvT, ikT, lfT = past

        cm_p, ct_p, crm_p, crt_p = _fox_c(gp, lf_m, lf_t)
        ct_s, crm_s, crt_s = _fox_c(gs, lfT, lf_t)
        ent = lambda name: (pr_m, pr_t, _cblk(name)[0])
        fox_t_p, fox_m = _attention('fox', gp, ent('fox_q'), ent('fox_k'), ent('fox_v'), None,
                                    fox=(cm_p, ct_p, crm_p, crt_p))
        fox_t_s, = _attention('fox', gs, ent('fox_q'), ent('fox_k'), ent('fox_v'), (fkT, fvT),
                              fox=(None, ct_s, crm_s, crt_s))
        dsa_args = ((iq_m, iq_t, 0), (pr_m, pr_t, _cblk('small')[0]), (ik_m, ik_t, 0))
        dsa_t_p, dsa_m = _attention('dsa', gp, (dq_m, dq_t, 0), (dk_m, dk_t, 0), ent('dsa_v'), None,
                                    dsa=dsa_args + (None,))
        dsa_t_s, = _attention('dsa', gs, (dq_m, dq_t, 0), (dk_m, dk_t, 0), ent('dsa_v'), (dkT, dvT),
                              dsa=dsa_args + (ikT,))
        wa2 = jnp.zeros((LANES, LANES), F32).at[_SMALL['gla_a']:_SMALL['gla_a'] + gla_w_a2.shape[1]].set(
            gla_w_a2[l]).astype(BF16)
        ba2 = gla_b_a2[l][None, :]
        gn = jnp.tile(gla_norm[l], HEADS)[None, :]
        eye = jnp.eye(HEADS, dtype=F32)
        s0 = (state_gla[l][:, :, :, None, :] * eye[None, :, None, :, None]).reshape(Bs, LANES, 256)
        gla_t_p, gla_m, gla_S_p = _gla(gp, pr_m, pr_t, None, wa2, ba2, gn)
        gla_t_s, gla_S_s = _gla(gs, pr_m, pr_t, s0, wa2, ba2, gn)
        dtb = _lane_vec(ssd_dt_bias[l], _DT_LANE)
        aneg = _lane_vec(-jnp.exp(ssd_a_log[l].astype(F32)), _DT_LANE)
        dvec = jnp.repeat(ssd_d[l].astype(F32), HD)[None, :]
        snorm = ssd_norm[l][None, :]
        grp = (jnp.arange(2)[:, None] == (jnp.arange(HEADS) // 2)[None, :]).astype(F32)
        hT = jnp.transpose(state_ssd[l], (0, 3, 1, 2))
        h0 = (hT[:, None] * grp[None, :, None, :, None]).reshape(Bs, LANES, 256)
        ssd_args = (ssd_conv_w[l], ssd_conv_b[l][None, :], dtb, aneg, dvec, snorm)
        ssd_t_p, ssd_m, ssd_H_p = _ssd(gp, pr_m, pr_t, None, None, *ssd_args)
        ssd_t_s, ssd_H_s = _ssd(gs, pr_m, pr_t, state_ssd_conv[l], h0, *ssd_args)

        cat = lambda a, b: jnp.concatenate([a, b], axis=0)
        ys_m = (fox_m, gla_m, dsa_m, ssd_m)
        ys_t = (cat(fox_t_p, fox_t_s), cat(gla_t_p, gla_t_s), cat(dsa_t_p, dsa_t_s), cat(ssd_t_p, ssd_t_s))
        wbr = jnp.stack([w_br_fox[l], w_br_gla[l], w_br_dsa[l], w_br_ssd[l]]).astype(BF16)
        wo = w_out[l].astype(BF16)
        h_m = _merge(ys_m, pr_m, h_m, wbr, wo, 256)
        h_t = _merge(ys_t, pr_t, h_t, wbr, wo, tm_t)
        last = l == depth - 1
        g_mlp = norm_mlp[l][None, :]
        wu = w_up[l].astype(BF16)
        wd = w_down[l].astype(BF16)
        gf = norm_final[None, :]
        res_m = _mlp(h_m, g_mlp, wu, wd, gf, last, tm_m)
        res_t = _mlp(h_t, g_mlp, wu, wd, gf, last, tm_t)
        h_m, h_t = res_m[0], res_t[0]
        if last:
            y_m, y_t = res_m[1], res_t[1]

        def seq_p(main, tail, w):
            return jnp.concatenate([tail[:Bp * n_meta].reshape(Bp, n_meta, w), main.reshape(Bp, Lmain, w)], axis=1)

        def seq_s(tail, w):
            return tail[Bp * n_meta:].reshape(Bs, Ls, w)

        def colm(name):
            o, w = _COL[name]
            return pr_m[:, o:o + w], pr_t[:, o:o + w], w
        Lp = n_meta + Lmain
        fk = colm('fox_k'); fv = colm('fox_v'); dv = colm('dsa_v'); xbc = colm('ssd_xbc')

        def gla_state(S, B):
            S4 = S.reshape(B, HEADS, GLA_DK, HEADS, 64)
            return jnp.stack([S4[:, hh, :, hh, :] for hh in range(HEADS)], axis=1)

        def ssd_state(H, B):
            H4 = H.reshape(B, 2, 64, HEADS, 64)
            return jnp.stack([jnp.swapaxes(H4[:, hh // 2, :, hh, :], 1, 2) for hh in range(HEADS)], axis=1)

        xbc_p = seq_p(xbc[0], xbc[1], 512)
        xbc_s = seq_s(xbc[1], 512)
        layer_p = (seq_p(fk[0], fk[1], 256).reshape(Bp, Lp, HEADS, HD),
                   seq_p(fv[0], fv[1], 256).reshape(Bp, Lp, HEADS, HD),
                   seq_p(lf_m, lf_t, LANES)[:, :, :HEADS],
                   seq_p(dk_m, dk_t, 256).reshape(Bp, Lp, HEADS, HD),
                   seq_p(dv[0], dv[1], 256).reshape(Bp, Lp, HEADS, HD),
                   seq_p(ik_m, ik_t, LANES)[:, :, :HD],
                   gla_state(gla_S_p, Bp), ssd_state(ssd_H_p, Bp),
                   xbc_p[:, Lp - (SSD_CONV - 1):])
        conv_s = jnp.concatenate([state_ssd_conv[l].astype(F32), xbc_s], axis=1)[:, Ls:]
        layer_s = (seq_s(fk[1], 256).reshape(Bs, Ls, HEADS, HD),
                   seq_s(fv[1], 256).reshape(Bs, Ls, HEADS, HD),
                   seq_s(lf_t, LANES)[:, :, :HEADS],
                   seq_s(dk_t, 256).reshape(Bs, Ls, HEADS, HD),
                   seq_s(dv[1], 256).reshape(Bs, Ls, HEADS, HD),
                   seq_s(ik_t, LANES)[:, :, :HD],
                   gla_state(gla_S_s, Bs), ssd_state(ssd_H_s, Bs),
                   conv_s)
        for k_ in range(9):
            outs_p[k_].append(layer_p[k_])
            outs_s[k_].append(layer_s[k_])

    y_prompt = y_m.reshape(Bp, Lmain, D)
    y_sample = y_t[Bp * n_meta:].reshape(Bs, Ls, D)
    return (y_prompt, y_sample) + tuple(jnp.stack(c) for c in outs_p) + tuple(jnp.stack(c) for c in outs_s)
```

```python
import functools
from typing import NamedTuple

import numpy as np
import jax
import jax.numpy as jnp
from jax import lax
from jax.experimental import pallas as pl
from jax.experimental.pallas import tpu as pltpu

F32 = jnp.float32
BF16 = jnp.bfloat16
I32 = jnp.int32

D_MODEL = 1024
N_META = 16
PAGE_SIZE = 128
CHUNK = 64
ROPE_THETA = 10000.0
EPS = 1e-6
NEG = -1e30
HEADS = 4
HD = 64
GLA_DK = 32
GLA_TAU = 16.0
DSA_TOPK = 256
SSD_CONV = 4
D_FF = 4 * D_MODEL

LANES = 128
VMEM_LIMIT = 56 * 1024 * 1024

_IN_SPLITS = (('fox_q', 256), ('fox_k', 256), ('fox_v', 256), ('fox_f', 4),
              ('gla_q', 128), ('gla_k', 128), ('gla_v', 256), ('gla_a', 16), ('gla_r', 256),
              ('dsa_q', 256), ('dsa_k', 256), ('dsa_v', 256),
              ('idx_q', 256), ('idx_w', 4), ('idx_k', 64),
              ('ssd_z', 256), ('ssd_xbc', 512), ('ssd_dt', 4),
              ('gates', 4096))
_COL = dict(gates=(0, 4096), ssd_xbc=(4096, 512), fox_q=(4608, 256), fox_k=(4864, 256), fox_v=(5120, 256),
            dsa_q=(5376, 256), dsa_k=(5632, 256), dsa_v=(5888, 256), idx_q=(6144, 256), ssd_z=(6400, 256),
            gla_v=(6656, 256), gla_r=(6912, 256), gla_q=(7168, 128), gla_k=(7296, 128),
            small=(7424, 128), idx_k=(7552, 128))
NP_COLS = 7680
_SMALL = dict(fox_f=0, idx_w=4, ssd_dt=8, gla_a=16)


def _cblk(name):
    off, w = _COL[name]
    assert off % w == 0
    return off // w, w


class Geom(NamedTuple):
    B: int
    Lm: int
    Lt: int
    Lpast: int
    tail_row0: int

    @property
    def km(self):
        return self.Lm if self.Lm else self.Lpast

    @property
    def pos_main0(self):
        return self.Lt if self.Lm else 0

    @property
    def pos_tail0(self):
        return 0 if self.Lm else self.Lpast

    @property
    def tail_blk0(self):
        assert self.tail_row0 % self.Lt == 0
        return self.tail_row0 // self.Lt


def _main_spec(g, w, cblk=0):
    return pl.BlockSpec((g.Lm, w), lambda b, *_: (b, cblk))


def _tail_spec(g, w, cblk=0):
    return pl.BlockSpec((g.Lt, w), lambda b, *_: (g.tail_blk0 + b, cblk))


def _params(sem):
    return pltpu.CompilerParams(dimension_semantics=sem, vmem_limit_bytes=VMEM_LIMIT)


def _log_sigmoid(x):
    return jnp.minimum(x, 0.0) - jnp.log1p(jnp.exp(-jnp.abs(x)))


def _softplus(x):
    return jnp.maximum(x, 0.0) + jnp.log1p(jnp.exp(-jnp.abs(x)))


def _silu(x):
    return x * jax.nn.sigmoid(x)


def _split3(x):
    hi = x.astype(BF16)
    r1 = x - hi.astype(F32)
    mid = r1.astype(BF16)
    lo = (r1 - mid.astype(F32)).astype(BF16)
    return hi, mid, lo


def _tri_left(tri, x):
    hi, mid, lo = _split3(x)
    d = functools.partial(jnp.dot, preferred_element_type=F32)
    return d(tri, hi) + d(tri, mid) + d(tri, lo)


def _tri_right(x, tri):
    hi, mid, lo = _split3(x)
    d = functools.partial(jnp.dot, preferred_element_type=F32)
    return d(hi, tri) + d(mid, tri) + d(lo, tri)


def _lower_tri(n, dtype=BF16):
    r = lax.broadcasted_iota(I32, (n, n), 0)
    c = lax.broadcasted_iota(I32, (n, n), 1)
    return (r >= c).astype(dtype)


def _upper_tri(n, dtype=BF16):
    r = lax.broadcasted_iota(I32, (n, n), 0)
    c = lax.broadcasted_iota(I32, (n, n), 1)
    return (r <= c).astype(dtype)


def _dot(a, b):
    return jnp.dot(a, b, preferred_element_type=F32)


def _dot_nt(a, b):
    return lax.dot_general(a, b, (((1,), (1,)), ((), ())), preferred_element_type=F32)


def _dot_tn(a, b):
    return lax.dot_general(a, b, (((0,), (0,)), ((), ())), preferred_element_type=F32)


def _head_of_lane(shape, width):
    return lax.broadcasted_iota(I32, shape, len(shape) - 1) // width


def _rms(x, g):
    ms = jnp.mean(x * x, axis=-1, keepdims=True)
    return x * lax.rsqrt(ms + EPS) * g


def _proj_body(x_ref, g_ref, w_ref, b_ref, o_ref):
    u = _rms(x_ref[...], g_ref[...]).astype(BF16)
    o_ref[...] = _dot(u, w_ref[...]) + b_ref[...]


def _proj(h, g, w, b, tm, tn):
    T, D = h.shape
    N = w.shape[1]
    return pl.pallas_call(
        _proj_body,
        grid=(N // tn, T // tm),
        in_specs=[pl.BlockSpec((tm, D), lambda j, i: (i, 0)),
                  pl.BlockSpec((1, D), lambda j, i: (0, 0)),
                  pl.BlockSpec((D, tn), lambda j, i: (0, j)),
                  pl.BlockSpec((1, tn), lambda j, i: (0, j))],
        out_specs=pl.BlockSpec((tm, tn), lambda j, i: (i, j)),
        out_shape=jax.ShapeDtypeStruct((T, N), F32),
        compiler_params=_params(("arbitrary", "arbitrary")),
        name="proj",
    )(h, g, w, b)


def _rope128(x, cos, sin_signed):
    lane = lax.broadcasted_iota(I32, x.shape, 1)
    swapped = jnp.where((lane % HD) < HD // 2, pltpu.roll(x, LANES - HD // 2, 1), pltpu.roll(x, HD // 2, 1))
    return x * cos + swapped * sin_signed


def _prep_body(dq_ref, dk_ref, iq_ref, ik_ref, sm_ref, cos_ref, sin_ref, odq, odk, oiq, oik, olf):
    cos = cos_ref[...]
    sin = sin_ref[...]
    for src, dst in ((dq_ref, odq), (dk_ref, odk), (iq_ref, oiq)):
        for half in range(2):
            sl = slice(half * LANES, (half + 1) * LANES)
            dst[:, sl] = _rope128(src[:, sl], cos, sin)
    oik[...] = _rope128(ik_ref[...], cos, sin)
    olf[...] = _log_sigmoid(sm_ref[...])


def _prep(pr, cos, sin, tm):
    T = pr.shape[0]

    def col(name):
        cb, w = _cblk(name)
        return pl.BlockSpec((tm, w), lambda i: (i, cb))
    row = lambda w: pl.BlockSpec((tm, w), lambda i: (i, 0))
    return pl.pallas_call(
        _prep_body,
        grid=(T // tm,),
        in_specs=[col('dsa_q'), col('dsa_k'), col('idx_q'), col('idx_k'), col('small'), row(LANES), row(LANES)],
        out_specs=[row(256), row(256), row(256), row(LANES), row(LANES)],
        out_shape=[jax.ShapeDtypeStruct((T, 256), F32)] * 3 + [jax.ShapeDtypeStruct((T, LANES), F32)] * 2,
        compiler_params=_params(("arbitrary",)),
        name="prep",
    )(pr, pr, pr, pr, pr, cos, sin)


def _merge_body(yf, yg, yd, ys, gates_ref, h_ref, wbr_ref, wout_ref, o_ref):
    merged = None
    for b, y in enumerate((yf, yg, yd, ys)):
        gate = jax.nn.sigmoid(gates_ref[:, b * D_MODEL:(b + 1) * D_MODEL])
        term = gate * _dot(y[...].astype(BF16), wbr_ref[b])
        merged = term if merged is None else merged + term
    o_ref[...] = h_ref[...] + _dot(merged.astype(BF16), wout_ref[...])


def _merge(ys, pr, h, wbr, wout, tm):
    T = h.shape[0]
    row = lambda w: pl.BlockSpec((tm, w), lambda i: (i, 0))
    return pl.pallas_call(
        _merge_body,
        grid=(T // tm,),
        in_specs=[row(256)] * 4 + [row(4 * D_MODEL), row(D_MODEL),
                                   pl.BlockSpec((4, 256, D_MODEL), lambda i: (0, 0, 0)),
                                   pl.BlockSpec((D_MODEL, D_MODEL), lambda i: (0, 0))],
        out_specs=row(D_MODEL),
        out_shape=jax.ShapeDtypeStruct((T, D_MODEL), F32),
        compiler_params=_params(("arbitrary",)),
        name="merge",
    )(*ys, pr, h, wbr, wout)


def _mlp_body(with_final, h_ref, g_ref, wup_ref, wdn_ref, gf_ref, o_ref, *maybe_y):
    h = h_ref[...]
    m = _rms(h, g_ref[...]).astype(BF16)
    acc = h
    for c in range(D_FF // D_MODEL):
        sl = slice(c * D_MODEL, (c + 1) * D_MODEL)
        hid = jnp.square(jnp.maximum(_dot(m, wup_ref[:, sl]), 0.0))
        acc = acc + _dot(hid.astype(BF16), wdn_ref[sl, :])
    o_ref[...] = acc
    if with_final:
        maybe_y[0][...] = _rms(acc, gf_ref[...])


def _mlp(h, g, wup, wdn, gf, with_final, tm):
    T = h.shape[0]
    row = pl.BlockSpec((tm, D_MODEL), lambda i: (i, 0))
    vec = pl.BlockSpec((1, D_MODEL), lambda i: (0, 0))
    n_out = 2 if with_final else 1
    outs = pl.pallas_call(
        functools.partial(_mlp_body, with_final),
        grid=(T // tm,),
        in_specs=[row, vec,
                  pl.BlockSpec((D_MODEL, D_FF), lambda i: (0, 0), pipeline_mode=pl.Buffered(1)),
                  pl.BlockSpec((D_FF, D_MODEL), lambda i: (0, 0), pipeline_mode=pl.Buffered(1)),
                  vec],
        out_specs=[row] * n_out,
        out_shape=[jax.ShapeDtypeStruct((T, D_MODEL), F32)] * n_out,
        compiler_params=_params(("arbitrary",)),
        name="mlp",
    )(h, g, wup, wdn, gf)
    return outs


def _foxc_body(g, *refs):
    if g.Lm:
        lfm, lft, ocm, oct_, orm, ort, pad = refs
    else:
        pt, lfm, lft, oct_, orm, ort, pad = refs
    nblk = g.km // LANES
    tl = _lower_tri(LANES)
    ort[...] = jnp.zeros_like(ort)
    pad[...] = jnp.zeros_like(pad)
    if g.Lm:
        ct = _tri_left(_lower_tri(g.Lt), lft[...])
        oct_[...] = ct
        pad[0:g.Lt, :] = ct
        ort[...] = pad[...].T[0:8, :]
        carry = ct[g.Lt - 1:g.Lt, :]
        for j in range(nblk):
            sl = slice(j * LANES, (j + 1) * LANES)
            c = _tri_left(tl, lfm[sl, :]) + carry
            ocm[sl, :] = c
            orm[:, sl] = c.T[0:8, :]
            carry = c[LANES - 1:LANES, :]
    else:
        tu = _upper_tri(LANES)
        carry = jnp.zeros((8, 1), F32)
        c = None
        seq = pl.program_id(0)
        for j in range(nblk):
            sl = slice(j * LANES, (j + 1) * LANES)
            c = _tri_right(lfm[pt[seq, j]], tu) + carry
            orm[:, sl] = c
            carry = c[:, LANES - 1:LANES]
        pad[0:8, :] = c
        carry_row = pad[...].T[LANES - 1:LANES, :]
        ct = _tri_left(_lower_tri(g.Lt), lft[...]) + carry_row
        oct_[...] = ct
        pad[0:8, :] = jnp.zeros((8, LANES), F32)
        pad[0:g.Lt, :] = ct
        ort[...] = pad[...].T[0:8, :]


def _fox_c(g, lf_main, lf_tail, paged=None):
    B = g.B
    outs_shape, outs_spec = [], []
    ins = [lf_main, lf_tail]
    if g.Lm:
        in_specs = [_main_spec(g, LANES), _tail_spec(g, LANES)]
        outs_shape.append(jax.ShapeDtypeStruct((B * g.Lm, LANES), F32))
        outs_spec.append(pl.BlockSpec((g.Lm, LANES), lambda b: (b, 0)))
    else:
        layer, page_table = paged
        ins = [page_table] + ins
        in_specs = [pl.BlockSpec(memory_space=pltpu.SMEM),
                    pl.BlockSpec((None,) + lf_main.shape[1:], lambda b: (layer, 0, 0, 0),
                                 pipeline_mode=pl.Buffered(1)),
                    _tail_spec(g, LANES)]
    outs_shape += [jax.ShapeDtypeStruct((B * g.Lt, LANES), F32),
                   jax.ShapeDtypeStruct((B, 8, g.km), F32),
                   jax.ShapeDtypeStruct((B, 8, LANES), F32)]
    outs_spec += [pl.BlockSpec((g.Lt, LANES), lambda b: (b, 0)),
                  pl.BlockSpec((None, 8, g.km), lambda b: (b, 0, 0)),
                  pl.BlockSpec((None, 8, LANES), lambda b: (b, 0, 0))]
    return pl.pallas_call(
        functools.partial(_foxc_body, g),
        grid=(B,),
        in_specs=in_specs,
        out_specs=outs_spec,
        out_shape=outs_shape,
        scratch_shapes=[pltpu.VMEM((LANES, LANES), F32)],
        compiler_params=_params(("arbitrary",)),
        name="fox_c",
    )(*ins)


def _sort_key(x):
    i = lax.bitcast_convert_type(x + 0.0, I32)
    return i ^ ((i >> 31) & jnp.int32(0x7FFFFFFF))


def _count(mask_m, mask_t):
    return (jnp.sum(mask_m.astype(I32), axis=-1, keepdims=True)
            + jnp.sum(mask_t.astype(I32), axis=-1, keepdims=True))


def _topk_mask(sc_m, sc_t, kpos_m, kpos_t, topk, max_pos, digit_bits):
    key_m = _sort_key(sc_m)
    key_t = _sort_key(sc_t)
    int_min = jnp.int32(-2 ** 31)

    if digit_bits == 1:
        n0 = _count(key_m >= 0, key_t >= 0)
        t0 = jnp.where(n0 >= topk, jnp.int32(0), int_min)

        def t_step(i, t):
            cand = t + lax.shift_left(jnp.int32(1), 30 - i)
            n = _count(key_m >= cand, key_t >= cand)
            return jnp.where(n >= topk, cand, t)
        t = lax.fori_loop(0, 31, t_step, t0)
    else:
        assert 32 % digit_bits == 0
        t = jnp.full((sc_m.shape[0], 1), int_min, I32)
        for k in range(32 // digit_bits):
            shift = 32 - digit_bits * (k + 1)
            n_ok = jnp.zeros_like(t)
            for c in range(1, 2 ** digit_bits):
                step = int(np.array(c << shift, np.uint32).astype(np.int32))
                n = _count(key_m >= t + step, key_t >= t + step)
                n_ok = n_ok + (n >= topk).astype(I32)
            t = t + lax.shift_left(n_ok, shift)

    gt_m, gt_t = key_m > t, key_t > t
    tie_m, tie_t = key_m == t, key_t == t
    n_gt = _count(gt_m, gt_t)
    need = topk - n_gt
    surplus = _count(tie_m, tie_t) - need
    nbits = int(max_pos).bit_length()

    def tie_cutoff():
        def p_step(i, p):
            cand = p + lax.shift_left(jnp.int32(1), nbits - 1 - i)
            n = _count(tie_m & (kpos_m < cand), tie_t & (kpos_t < cand))
            return jnp.where(n < need, cand, p)
        return lax.fori_loop(0, nbits, p_step, jnp.zeros_like(need))

    def take_all():
        return jnp.full(need.shape, 2 ** nbits - 1, I32)
    p = lax.cond(jnp.max(surplus) > 0, tie_cutoff, take_all)
    return gt_m | (tie_m & (kpos_m <= p)), gt_t | (tie_t & (kpos_t <= p))


def _attn_body(kind, g, topk, layer, *refs):
    refs = list(refs)
    take = lambda n: [refs.pop(0) for _ in range(n)]
    has_q_main = bool(g.Lm)
    R = {}
    if not has_q_main:
        R['pt'], = take(1)
    R['qt'], = take(1)
    if has_q_main:
        R['qm'], = take(1)
    R['kt'], R['vt'], R['km'], R['vm'] = take(4)
    if kind == 'fox':
        R['cqt'], = take(1)
        if has_q_main:
            R['cqm'], = take(1)
        R['crm'], R['crt'] = take(2)
    else:
        R['iqt'], R['wt'] = take(2)
        if has_q_main:
            R['iqm'], R['wm'] = take(2)
        R['ikt'], R['ikm'] = take(2)
    R['ot'], = take(1)
    if has_q_main:
        R['om'], = take(1)
    R['KT'], R['VT'], R['ktp'], R['vtp'] = take(4)
    if kind == 'dsa':
        R['IKT'], R['iktp'] = take(2)
    if not has_q_main:
        R['kbuf'], R['vbuf'] = take(2)
        if kind == 'dsa':
            R['ikbuf'], = take(1)
        R['sem'], = take(1)
    assert not refs

    KM, Lt = g.km, g.Lt
    i = pl.program_id(1)
    n_pages = KM // PAGE_SIZE

    if not has_q_main:
        b = pl.program_id(0)
        paged = [(R['km'], R['kbuf']), (R['vm'], R['vbuf'])]
        if kind == 'dsa':
            paged.append((R['ikm'], R['ikbuf']))

        def page_copies(seq, slot):
            for t, (pool, buf) in enumerate(paged):
                for p in range(n_pages):
                    yield pltpu.make_async_copy(pool.at[layer, R['pt'][seq, p]], buf.at[slot, p],
                                                R['sem'].at[slot, t, p])

        @pl.when(b == 0)
        def _first_fetch():
            for c in page_copies(0, 0):
                c.start()

        @pl.when(b + 1 < g.B)
        def _next_fetch():
            for c in page_copies(b + 1, (b + 1) % 2):
                c.start()
        slot = b % 2
        for c in page_copies(b, slot):
            c.wait()

    @pl.when(i == 0)
    def _init():
        if has_q_main:
            for j in range(KM // 256):
                sl = slice(j * 256, (j + 1) * 256)
                kT = R['km'][sl, :].T.astype(BF16)
                vT = R['vm'][sl, :].T.astype(BF16)
                for h in range(HEADS):
                    R['KT'][h, :, sl] = kT[h * HD:(h + 1) * HD, :]
                    R['VT'][h, :, sl] = vT[h * HD:(h + 1) * HD, :]
            if kind == 'dsa':
                for j in range(KM // LANES):
                    sl = slice(j * LANES, (j + 1) * LANES)
                    R['IKT'][:, sl] = R['ikm'][sl, :].T[0:HD, :].astype(BF16)
        else:
            for p in range(n_pages):
                sl = slice(p * PAGE_SIZE, (p + 1) * PAGE_SIZE)
                for h in range(HEADS):
                    R['KT'][h, :, sl] = R['kbuf'][slot, p, h].astype(BF16)
                    R['VT'][h, :, sl] = R['vbuf'][slot, p, h].astype(BF16)
                if kind == 'dsa':
                    R['IKT'][:, sl] = R['ikbuf'][slot, p].astype(BF16)
        R['ktp'][...] = jnp.zeros_like(R['ktp'])
        R['vtp'][...] = jnp.zeros_like(R['vtp'])
        R['ktp'][0:Lt, :] = R['kt'][...].astype(BF16)
        R['vtp'][0:Lt, :] = R['vt'][...].astype(BF16)
        if kind == 'dsa':
            R['iktp'][...] = jnp.zeros_like(R['iktp'])
            R['iktp'][0:Lt, :] = R['ikt'][...].astype(BF16)

    def process(rows, q, extra, qpos0, out_ref, out_rows, kw):
        qpos = qpos0 + lax.broadcasted_iota(I32, (rows, 1), 0)
        lane_t = lax.broadcasted_iota(I32, (1, LANES), 1)
        kpos_t = g.pos_tail0 + lane_t
        mask_t = (kpos_t <= qpos) & (lane_t < Lt)
        if kw:
            kpos_m = g.pos_main0 + lax.broadcasted_iota(I32, (1, kw), 1)
            mask_m = kpos_m <= qpos
        if kind == 'dsa':
            iq, w = extra
        else:
            cq, = extra
        if kind == 'dsa' and kw + Lt > topk:
            assert kw
            sc_m = jnp.zeros((rows, kw), F32)
            sc_t = jnp.zeros((rows, LANES), F32)
            for h in range(HEADS):
                iqh = iq[:, h * HD:(h + 1) * HD].astype(BF16)
                wh = w[:, _SMALL['idx_w'] + h:_SMALL['idx_w'] + h + 1]
                sc_m = sc_m + wh * jnp.maximum(_dot(iqh, R['IKT'][:, 0:kw]), 0.0)
                sc_t = sc_t + wh * jnp.maximum(_dot_nt(iqh, R['iktp'][:, 0:HD]), 0.0)
            sc_m = jnp.where(mask_m, sc_m, NEG)
            sc_t = jnp.where(lane_t < Lt, jnp.where(mask_t, sc_t, NEG), -jnp.inf)
            sel_m, sel_t = _topk_mask(sc_m, sc_t, kpos_m, kpos_t, topk, g.km + g.Lt, 1 if rows >= 64 else 4)
            mask_m = mask_m & sel_m
            mask_t = mask_t & sel_t
        for h in range(HEADS):
            hs = slice(h * HD, (h + 1) * HD)
            qh = q[:, hs].astype(BF16)
            s_t = _dot_nt(qh, R['ktp'][:, hs]) * (HD ** -0.5)
            if kind == 'fox':
                cqh = cq[:, h:h + 1]
                s_t = s_t + cqh - R['crt'][h:h + 1, :]
            s_t = jnp.where(mask_t, s_t, NEG)
            mx = jnp.max(s_t, axis=-1, keepdims=True)
            if kw:
                s_m = _dot(qh, R['KT'][h, :, 0:kw]) * (HD ** -0.5)
                if kind == 'fox':
                    s_m = s_m + cqh - R['crm'][h:h + 1, 0:kw]
                s_m = jnp.where(mask_m, s_m, NEG)
                mx = jnp.maximum(mx, jnp.max(s_m, axis=-1, keepdims=True))
            p_t = jnp.exp(s_t - mx)
            den = jnp.sum(p_t, axis=-1, keepdims=True)
            o = _dot(p_t.astype(BF16), R['vtp'][:, hs])
            if kw:
                p_m = jnp.exp(s_m - mx)
                den = den + jnp.sum(p_m, axis=-1, keepdims=True)
                o = o + _dot_nt(p_m.astype(BF16), R['VT'][h, :, 0:kw])
            out_ref[out_rows, hs] = o / den

    def extras(suffix, rows_sl):
        if kind == 'fox':
            return (R['cq' + suffix][rows_sl, :],)
        return (R['iq' + suffix][rows_sl, :], R['w' + suffix][rows_sl, :])

    @pl.when(i == 0)
    def _tail_queries():
        all_rows = slice(None)
        process(Lt, R['qt'][...], extras('t', all_rows), g.pos_tail0, R['ot'], all_rows, 0 if has_q_main else KM)

    if has_q_main:
        nqb = g.Lm // LANES
        per = 4 if (nqb % 4 == 0 and nqb > 4) else nqb
        for v in range(nqb // per):
            @pl.when((i > v * per) & (i <= (v + 1) * per))
            def _main_queries(kw=(v + 1) * per * LANES):
                r0 = pl.multiple_of((i - 1) * LANES, LANES)
                rows_sl = pl.ds(r0, LANES)
                process(LANES, R['qm'][rows_sl, :], extras('m', rows_sl), g.pos_main0 + r0, R['om'], rows_sl, kw)


def _attention(kind, g, q, k, v, paged=None, fox=None, dsa=None):
    B, KM, Lt = g.B, g.km, g.Lt
    has_q_main = bool(g.Lm)
    nq = 1 + (g.Lm // LANES if has_q_main else 0)
    ins, specs = [], []
    any_spec = pl.BlockSpec(memory_space=pl.ANY)
    layer = 0

    def add(arr, spec):
        ins.append(arr)
        specs.append(spec)

    if not has_q_main:
        layer, page_table, k_pool, v_pool = paged
        add(page_table, pl.BlockSpec(memory_space=pltpu.SMEM))

    def add_tok(entry, w, main=True, tail=True):
        arr_m, arr_t, cb = entry
        if tail:
            add(arr_t, _tail_spec(g, w, cb))
        if main and has_q_main:
            add(arr_m, _main_spec(g, w, cb))

    add_tok(q, 256)
    add_tok(k, 256, main=False)
    add_tok(v, 256, main=False)
    if has_q_main:
        add(k[0], _main_spec(g, 256, k[2]))
        add(v[0], _main_spec(g, 256, v[2]))
    else:
        add(k_pool, any_spec)
        add(v_pool, any_spec)
    if kind == 'fox':
        c_main_col, c_tail_col, c_main_row, c_tail_row = fox
        add(c_tail_col, pl.BlockSpec((Lt, LANES), lambda b, i: (b, 0)))
        if has_q_main:
            add(c_main_col, pl.BlockSpec((g.Lm, LANES), lambda b, i: (b, 0)))
        add(c_main_row, pl.BlockSpec((None, 8, KM), lambda b, i: (b, 0, 0)))
        add(c_tail_row, pl.BlockSpec((None, 8, LANES), lambda b, i: (b, 0, 0)))
    else:
        iq, w, ik, ik_pool = dsa
        add(iq[1], _tail_spec(g, 256, iq[2]))
        add(w[1], _tail_spec(g, LANES, w[2]))
        if has_q_main:
            add(iq[0], _main_spec(g, 256, iq[2]))
            add(w[0], _main_spec(g, LANES, w[2]))
        add(ik[1], _tail_spec(g, LANES, ik[2]))
        if has_q_main:
            add(ik[0], _main_spec(g, LANES, ik[2]))
        else:
            add(ik_pool, any_spec)

    out_shape = [jax.ShapeDtypeStruct((B * Lt, 256), F32)]
    out_specs = [pl.BlockSpec((Lt, 256), lambda b, i: (b, 0))]
    if has_q_main:
        out_shape.append(jax.ShapeDtypeStruct((B * g.Lm, 256), F32))
        out_specs.append(pl.BlockSpec((g.Lm, 256), lambda b, i: (b, 0)))
    scratch = [pltpu.VMEM((HEADS, HD, KM), BF16), pltpu.VMEM((HEADS, HD, KM), BF16),
               pltpu.VMEM((LANES, 256), BF16), pltpu.VMEM((LANES, 256), BF16)]
    if kind == 'dsa':
        scratch += [pltpu.VMEM((HD, KM), BF16), pltpu.VMEM((LANES, LANES), BF16)]
    if not has_q_main:
        n_pages = KM // PAGE_SIZE
        scratch += [pltpu.VMEM((2, n_pages, HEADS, HD, PAGE_SIZE), F32)] * 2
        if kind == 'dsa':
            scratch.append(pltpu.VMEM((2, n_pages, HD, PAGE_SIZE), F32))
        scratch.append(pltpu.SemaphoreType.DMA((2, 3 if kind == 'dsa' else 2, n_pages)))
    topk = min(DSA_TOPK, (KM + Lt) // 4)
    outs = pl.pallas_call(
        functools.partial(_attn_body, kind, g, topk, layer),
        grid=(B, nq),
        in_specs=specs,
        out_specs=out_specs,
        out_shape=out_shape,
        scratch_shapes=scratch,
        compiler_params=_params(("arbitrary", "arbitrary")),
        name=kind + "_attn",
    )(*ins)
    return outs


def _gla_chunk(c, q, k, v, r, small, S, wa2, ba2, gnorm):
    logit = _dot(small.astype(BF16), wa2) + ba2
    gate = _log_sigmoid(logit) / GLA_TAU
    b = _tri_left(_lower_tri(c), gate)
    bT = b.T
    kT = k.T
    qs = q * (GLA_DK ** -0.5)
    row = lax.broadcasted_iota(I32, (c, c), 0)
    col = lax.broadcasted_iota(I32, (c, c), 1)
    causal = row >= col
    head256 = _head_of_lane((c, 256), 64)
    vb = v.astype(BF16)
    y = _dot((qs * jnp.exp(b)).astype(BF16), S.astype(BF16))
    for h in range(HEADS):
        att = jnp.zeros((c, c), F32)
        for d in range(GLA_DK):
            j = h * GLA_DK + d
            decay = jnp.exp(jnp.minimum(b[:, j:j + 1] - bT[j:j + 1, :], 0.0))
            att = att + (qs[:, j:j + 1] * kT[j:j + 1, :]) * decay
        att = jnp.where(causal, att, 0.0)
        y = y + jnp.where(head256 == h, _dot(att.astype(BF16), vb), 0.0)
    b_last_row = b[c - 1:c, :]
    b_last_col = bT[:, c - 1:c]
    kd = (k * jnp.exp(b_last_row - b)).astype(BF16)
    upd = _dot_tn(kd, vb)
    diag = (lax.broadcasted_iota(I32, (LANES, 256), 0) // GLA_DK) == _head_of_lane((LANES, 256), 64)
    S_new = jnp.exp(b_last_col) * S + jnp.where(diag, upd, 0.0)
    inv = jnp.zeros((c, 256), F32)
    for h in range(HEADS):
        ms = jnp.sum(jnp.where(head256 == h, y * y, 0.0), axis=-1, keepdims=True) / 64.0
        inv = jnp.where(head256 == h, lax.rsqrt(ms + EPS), inv)
    out = (y * inv * gnorm) * _silu(r)
    return out, S_new


def _gla_body(g, *refs):
    refs = list(refs)
    take = lambda n: [refs.pop(0) for _ in range(n)]
    qt, kt, vt, rt, st = take(5)
    if g.Lm:
        qm, km, vm, rm, sm = take(5)
    else:
        s0, = take(1)
    wa2, ba2, gnorm = take(3)
    yt, = take(1)
    if g.Lm:
        ym, = take(1)
    s_out, = take(1)
    assert not refs
    w = wa2[...]
    bb = ba2[...]
    gn = gnorm[...]
    S = jnp.zeros((LANES, 256), F32) if g.Lm else s0[...]
    out, S = _gla_chunk(g.Lt, qt[...], kt[...], vt[...], rt[...], st[...], S, w, bb, gn)
    yt[...] = out
    if g.Lm:
        s_out[...] = S

        def step(ci, carry):
            sl = pl.ds(pl.multiple_of(ci * CHUNK, CHUNK), CHUNK)
            o, s_new = _gla_chunk(CHUNK, qm[sl, :], km[sl, :], vm[sl, :], rm[sl, :], sm[sl, :], s_out[...], w, bb, gn)
            ym[sl, :] = o
            s_out[...] = s_new
            return carry
        lax.fori_loop(0, g.Lm // CHUNK, step, 0)
    else:
        s_out[...] = S


def _gla(g, pr_m, pr_t, s0, wa2, ba2, gnorm):
    B = g.B
    ins, specs = [], []
    for name in ('gla_q', 'gla_k', 'gla_v', 'gla_r', 'small'):
        cb, w = _cblk(name)
        ins.append(pr_t)
        specs.append(_tail_spec(g, w, cb))
    if g.Lm:
        for name in ('gla_q', 'gla_k', 'gla_v', 'gla_r', 'small'):
            cb, w = _cblk(name)
            ins.append(pr_m)
            specs.append(_main_spec(g, w, cb))
    else:
        ins.append(s0)
        specs.append(pl.BlockSpec((None, LANES, 256), lambda b: (b, 0, 0)))
    ins += [wa2, ba2, gnorm]
    specs += [pl.BlockSpec((LANES, LANES), lambda b: (0, 0)), pl.BlockSpec((1, LANES), lambda b: (0, 0)),
              pl.BlockSpec((1, 256), lambda b: (0, 0))]
    out_shape = [jax.ShapeDtypeStruct((B * g.Lt, 256), F32)]
    out_specs = [pl.BlockSpec((g.Lt, 256), lambda b: (b, 0))]
    if g.Lm:
        out_shape.append(jax.ShapeDtypeStruct((B * g.Lm, 256), F32))
        out_specs.append(pl.BlockSpec((g.Lm, 256), lambda b: (b, 0)))
    out_shape.append(jax.ShapeDtypeStruct((B, LANES, 256), F32))
    out_specs.append(pl.BlockSpec((None, LANES, 256), lambda b: (b, 0, 0)))
    return pl.pallas_call(
        functools.partial(_gla_body, g),
        grid=(B,),
        in_specs=specs,
        out_specs=out_specs,
        out_shape=out_shape,
        compiler_params=_params(("arbitrary",)),
        name="gla",
    )(*ins)


_DT_LANE = _SMALL['ssd_dt']


def _expand_heads(x, c):
    head256 = _head_of_lane((c, 256), 64)
    out = jnp.zeros((c, 256), F32)
    for h in range(HEADS):
        out = jnp.where(head256 == h, x[:, _DT_LANE + h:_DT_LANE + h + 1], out)
    return out


def _ssd_chunk(c, xbc, z, small, H, dtb, aneg, dvec, norm):
    x = xbc[:, 0:256]
    Bm = xbc[:, 256:384]
    Cm = xbc[:, 384:512]
    dt = _softplus(small + dtb)
    a = dt * aneg
    cum = _tri_left(_lower_tri(c), a)
    cumT = cum.T
    dtT = dt.T
    row = lax.broadcasted_iota(I32, (c, c), 0)
    col = lax.broadcasted_iota(I32, (c, c), 1)
    causal = row >= col
    head256 = _head_of_lane((c, 256), 64)
    xb = x.astype(BF16)
    Cb = Cm.astype(BF16)
    Bb = Bm.astype(BF16)
    y = _dot(Cb, H.astype(BF16)) * _expand_heads(jnp.exp(cum), c)
    cb = [_dot_nt(Cb[:, gi * 64:(gi + 1) * 64], Bb[:, gi * 64:(gi + 1) * 64]) for gi in range(2)]
    for h in range(HEADS):
        j = _DT_LANE + h
        seg = jnp.exp(jnp.minimum(cum[:, j:j + 1] - cumT[j:j + 1, :], 0.0))
        m = jnp.where(causal, cb[h // 2] * seg, 0.0) * dtT[j:j + 1, :]
        y = y + jnp.where(head256 == h, _dot(m.astype(BF16), xb), 0.0)
    last_row = cum[c - 1:c, :]
    wgt = jnp.exp(last_row - cum) * dt
    xw = (x * _expand_heads(wgt, c)).astype(BF16)
    upd = _dot_tn(Bb, xw)
    same_group = (lax.broadcasted_iota(I32, (LANES, 256), 0) // 64) == (_head_of_lane((LANES, 256), 64) // 2)
    decay = _expand_heads(jnp.exp(last_row), 1)
    H_new = decay * H + jnp.where(same_group, upd, 0.0)
    sy = y + dvec * x
    gated = sy * _silu(z)
    out = _rms(gated, norm)
    return out, H_new


def _ssd_body(g, *refs):
    refs = list(refs)
    take = lambda n: [refs.pop(0) for _ in range(n)]
    xt, zt, st = take(3)
    if g.Lm:
        xm, zm, sm = take(3)
    else:
        buf0, h0 = take(2)
    cw, cb, dtb, aneg, dvec, norm = take(6)
    yt, = take(1)
    if g.Lm:
        ym, = take(1)
    h_out, = take(1)
    xin, xc = take(2)
    assert not refs
    Lt, Lm = g.Lt, g.Lm
    L = Lt + Lm
    if g.Lm:
        xin[0:8, :] = jnp.zeros((8, 512), F32)
    else:
        xin[0:8, :] = jnp.zeros((8, 512), F32)
        xin[5:8, :] = buf0[...]
    xin[8:8 + Lt, :] = xt[...]
    if g.Lm:
        xin[8 + Lt:8 + L, :] = xm[...]
    w = cw[...]
    bias = cb[...]

    def conv(blk):
        n = blk.shape[0] - 8
        acc = bias + blk[5:5 + n] * w[0:1, :]
        acc = acc + blk[6:6 + n] * w[1:2, :]
        acc = acc + blk[7:7 + n] * w[2:3, :]
        acc = acc + blk[8:8 + n] * w[3:4, :]
        return _silu(acc)

    xc[0:Lt, :] = conv(xin[0:8 + Lt, :])
    if g.Lm:
        def cstep(ci, carry):
            r0 = pl.multiple_of(Lt + ci * LANES, 8)
            xc[pl.ds(r0, LANES), :] = conv(xin[pl.ds(r0, LANES + 8), :])
            return carry
        lax.fori_loop(0, Lm // LANES, cstep, 0)

    consts = (dtb[...], aneg[...], dvec[...], norm[...])
    H = jnp.zeros((LANES, 256), F32) if g.Lm else h0[...]
    out, H = _ssd_chunk(Lt, xc[0:Lt, :], zt[...], st[...], H, *consts)
    yt[...] = out
    h_out[...] = H
    if g.Lm:
        def step(ci, carry):
            sl = pl.ds(pl.multiple_of(ci * CHUNK, CHUNK), CHUNK)
            slc = pl.ds(pl.multiple_of(Lt + ci * CHUNK, 8), CHUNK)
            o, h_new = _ssd_chunk(CHUNK, xc[slc, :], zm[sl, :], sm[sl, :], h_out[...], *consts)
            ym[sl, :] = o
            h_out[...] = h_new
            return carry
        lax.fori_loop(0, Lm // CHUNK, step, 0)


def _ssd(g, pr_m, pr_t, buf0, h0, cw, cb, dtb, aneg, dvec, norm):
    B = g.B
    ins, specs = [], []
    names = ('ssd_xbc', 'ssd_z', 'small')
    for name in names:
        cbk, w = _cblk(name)
        ins.append(pr_t)
        specs.append(_tail_spec(g, w, cbk))
    if g.Lm:
        for name in names:
            cbk, w = _cblk(name)
            ins.append(pr_m)
            specs.append(_main_spec(g, w, cbk))
    else:
        ins += [buf0, h0]
        specs += [pl.BlockSpec((None, SSD_CONV - 1, 512), lambda b: (b, 0, 0)),
                  pl.BlockSpec((None, LANES, 256), lambda b: (b, 0, 0))]
    ins += [cw, cb, dtb, aneg, dvec, norm]
    specs += [pl.BlockSpec((SSD_CONV, 512), lambda b: (0, 0)), pl.BlockSpec((1, 512), lambda b: (0, 0)),
              pl.BlockSpec((1, LANES), lambda b: (0, 0)), pl.BlockSpec((1, LANES), lambda b: (0, 0)),
              pl.BlockSpec((1, 256), lambda b: (0, 0)), pl.BlockSpec((1, 256), lambda b: (0, 0))]
    out_shape = [jax.ShapeDtypeStruct((B * g.Lt, 256), F32)]
    out_specs = [pl.BlockSpec((g.Lt, 256), lambda b: (b, 0))]
    if g.Lm:
        out_shape.append(jax.ShapeDtypeStruct((B * g.Lm, 256), F32))
        out_specs.append(pl.BlockSpec((g.Lm, 256), lambda b: (b, 0)))
    out_shape.append(jax.ShapeDtypeStruct((B, LANES, 256), F32))
    out_specs.append(pl.BlockSpec((None, LANES, 256), lambda b: (b, 0, 0)))
    L = g.Lt + g.Lm
    return pl.pallas_call(
        functools.partial(_ssd_body, g),
        grid=(B,),
        in_specs=specs,
        out_specs=out_specs,
        out_shape=out_shape,
        scratch_shapes=[pltpu.VMEM((8 + L, 512), F32), pltpu.VMEM((L, 512), F32)],
        compiler_params=_params(("arbitrary",)),
        name="ssd",
    )(*ins)


def _pad_cols(w, b):
    src = {}
    off = 0
    for name, wd in _IN_SPLITS:
        src[name] = (off, wd)
        off += wd
    wb = jnp.concatenate([w, b[None, :]], axis=0)
    pieces = []
    pos = 0

    def emit(name, dst):
        nonlocal pos
        if dst > pos:
            pieces.append(jnp.zeros((wb.shape[0], dst - pos), wb.dtype))
        o, wd = src[name]
        pieces.append(wb[:, o:o + wd])
        pos = dst + wd
    for name, (dst, _) in sorted(_COL.items(), key=lambda kv: kv[1][0]):
        if name == 'small':
            for sname, lane in sorted(_SMALL.items(), key=lambda kv: kv[1]):
                emit(sname, dst + lane)
        else:
            emit(name, dst)
    if pos < NP_COLS:
        pieces.append(jnp.zeros((wb.shape[0], NP_COLS - pos), wb.dtype))
    wbp = jnp.concatenate(pieces, axis=1)
    return wbp[:-1].astype(BF16), wbp[-1:]


def _rope_tables(pos):
    half = HD // 2
    freqs = ROPE_THETA ** (-jnp.arange(half, dtype=F32) / half)
    ang = pos.astype(F32)[:, None] * freqs[None, :]
    cos = jnp.cos(ang)
    sin = jnp.sin(ang)
    cos4 = jnp.concatenate([cos, cos, cos, cos], axis=1)
    sin4 = jnp.concatenate([-sin, sin, -sin, sin], axis=1)
    return cos4, sin4


def _lane_vec(values, lane0, width=LANES):
    return jnp.zeros((1, width), F32).at[0, lane0:lane0 + values.shape[0]].set(values.astype(F32))


def kernel(x_prompt, x_sample, cache_fox_k, cache_fox_v, cache_fox_logf, cache_dsa_k, cache_dsa_v, cache_dsa_kidx,
           state_gla, state_ssd, state_ssd_conv, page_table, meta, w_in, b_in, norm_mix, gla_w_a2, gla_b_a2,
           gla_norm, ssd_conv_w, ssd_conv_b, ssd_dt_bias, ssd_a_log, ssd_d, ssd_norm, w_br_fox, w_br_gla,
           w_br_dsa, w_br_ssd, w_out, norm_mlp, w_up, w_down, norm_final):
    depth = w_in.shape[0]
    Bp, Lmain, D = x_prompt.shape
    Bs, Ls, _ = x_sample.shape
    n_pages = page_table.shape[1]
    Lpast = n_pages * cache_fox_k.shape[2]
    n_meta = meta.shape[0]
    gp = Geom(B=Bp, Lm=Lmain, Lt=n_meta, Lpast=0, tail_row0=0)
    gs = Geom(B=Bs, Lm=0, Lt=Ls, Lpast=Lpast, tail_row0=Bp * n_meta)
    Tm = Bp * Lmain
    Tt = Bp * n_meta + Bs * Ls

    h_m = x_prompt.reshape(Tm, D)
    h_t = jnp.concatenate([jnp.broadcast_to(meta.astype(F32)[None], (Bp, n_meta, D)).reshape(Bp * n_meta, D),
                           x_sample.reshape(Bs * Ls, D)], axis=0)
    pos_m = jnp.tile(n_meta + jnp.arange(Lmain, dtype=I32), Bp)
    pos_t = jnp.concatenate([jnp.tile(jnp.arange(n_meta, dtype=I32), Bp),
                             jnp.tile(Lpast + jnp.arange(Ls, dtype=I32), Bs)])
    cos_m, sin_m = _rope_tables(pos_m)
    cos_t, sin_t = _rope_tables(pos_t)

    pools_kv = [jnp.transpose(c, (0, 1, 3, 4, 2)) for c in (cache_fox_k, cache_fox_v, cache_dsa_k, cache_dsa_v)]
    pool_kidx = jnp.transpose(cache_dsa_kidx, (0, 1, 3, 2))
    pool_logf = jnp.pad(jnp.transpose(cache_fox_logf, (0, 1, 3, 2)), ((0, 0), (0, 0), (0, 8 - HEADS), (0, 0)))

    tm_m = 512
    tm_t = Tt // 3 if (Tt % 3 == 0 and (Tt // 3) % 8 == 0) else Tt
    tn = 1536
    outs_p = [[] for _ in range(9)]
    outs_s = [[] for _ in range(9)]
    y_m = y_t = None
    for l in range(depth):
        wp, bp = _pad_cols(w_in[l], b_in[l])
        g_mix = norm_mix[l][None, :]
        pr_m = _proj(h_m, g_mix, wp, bp, tm_m, tn)
        pr_t = _proj(h_t, g_mix, wp, bp, tm_t, tn)
        dq_m, dk_m, iq_m, ik_m, lf_m = _prep(pr_m, cos_m, sin_m, tm_m)
        dq_t, dk_t, iq_t, ik_t, lf_t = _prep(pr_t, cos_t, sin_t, tm_t)

        fox_kT, fox_vT, dsa_kT, dsa_vT = pools_kv

        cm_p, ct_p, crm_p, crt_p = _fox_c(gp, lf_m, lf_t)
        ct_s, crm_s, crt_s = _fox_c(gs, pool_logf, lf_t, paged=(l, page_table))
        ent = lambda name: (pr_m, pr_t, _cblk(name)[0])
        fox_t_p, fox_m = _attention('fox', gp, ent('fox_q'), ent('fox_k'), ent('fox_v'),
                                    fox=(cm_p, ct_p, crm_p, crt_p))
        fox_t_s, = _attention('fox', gs, ent('fox_q'), ent('fox_k'), ent('fox_v'),
                              paged=(l, page_table, fox_kT, fox_vT), fox=(None, ct_s, crm_s, crt_s))
        dsa_args = ((iq_m, iq_t, 0), (pr_m, pr_t, _cblk('small')[0]), (ik_m, ik_t, 0))
        dsa_t_p, dsa_m = _attention('dsa', gp, (dq_m, dq_t, 0), (dk_m, dk_t, 0), ent('dsa_v'),
                                    dsa=dsa_args + (None,))
        dsa_t_s, = _attention('dsa', gs, (dq_m, dq_t, 0), (dk_m, dk_t, 0), ent('dsa_v'),
                              paged=(l, page_table, dsa_kT, dsa_vT), dsa=dsa_args + (pool_kidx,))
        wa2 = jnp.zeros((LANES, LANES), F32).at[_SMALL['gla_a']:_SMALL['gla_a'] + gla_w_a2.shape[1]].set(
            gla_w_a2[l]).astype(BF16)
        ba2 = gla_b_a2[l][None, :]
        gn = jnp.tile(gla_norm[l], HEADS)[None, :]
        eye = jnp.eye(HEADS, dtype=F32)
        s0 = (state_gla[l][:, :, :, None, :] * eye[None, :, None, :, None]).reshape(Bs, LANES, 256)
        gla_t_p, gla_m, gla_S_p = _gla(gp, pr_m, pr_t, None, wa2, ba2, gn)
        gla_t_s, gla_S_s = _gla(gs, pr_m, pr_t, s0, wa2, ba2, gn)
        dtb = _lane_vec(ssd_dt_bias[l], _DT_LANE)
        aneg = _lane_vec(-jnp.exp(ssd_a_log[l].astype(F32)), _DT_LANE)
        dvec = jnp.repeat(ssd_d[l].astype(F32), HD)[None, :]
        snorm = ssd_norm[l][None, :]
        grp = (jnp.arange(2)[:, None] == (jnp.arange(HEADS) // 2)[None, :]).astype(F32)
        hT = jnp.transpose(state_ssd[l], (0, 3, 1, 2))
        h0 = (hT[:, None] * grp[None, :, None, :, None]).reshape(Bs, LANES, 256)
        ssd_args = (ssd_conv_w[l], ssd_conv_b[l][None, :], dtb, aneg, dvec, snorm)
        ssd_t_p, ssd_m, ssd_H_p = _ssd(gp, pr_m, pr_t, None, None, *ssd_args)
        ssd_t_s, ssd_H_s = _ssd(gs, pr_m, pr_t, state_ssd_conv[l], h0, *ssd_args)

        cat = lambda a, b: jnp.concatenate([a, b], axis=0)
        ys_m = (fox_m, gla_m, dsa_m, ssd_m)
        ys_t = (cat(fox_t_p, fox_t_s), cat(gla_t_p, gla_t_s), cat(dsa_t_p, dsa_t_s), cat(ssd_t_p, ssd_t_s))
        wbr = jnp.stack([w_br_fox[l], w_br_gla[l], w_br_dsa[l], w_br_ssd[l]]).astype(BF16)
        wo = w_out[l].astype(BF16)
        h_m = _merge(ys_m, pr_m, h_m, wbr, wo, 256)
        h_t = _merge(ys_t, pr_t, h_t, wbr, wo, tm_t)
        last = l == depth - 1
        g_mlp = norm_mlp[l][None, :]
        wu = w_up[l].astype(BF16)
        wd = w_down[l].astype(BF16)
        gf = norm_final[None, :]
        res_m = _mlp(h_m, g_mlp, wu, wd, gf, last, tm_m)
        res_t = _mlp(h_t, g_mlp, wu, wd, gf, last, tm_t)
        h_m, h_t = res_m[0], res_t[0]
        if last:
            y_m, y_t = res_m[1], res_t[1]

        def seq_p(main, tail, w):
            return jnp.concatenate([tail[:Bp * n_meta].reshape(Bp, n_meta, w), main.reshape(Bp, Lmain, w)], axis=1)

        def seq_s(tail, w):
            return tail[Bp * n_meta:].reshape(Bs, Ls, w)

        def colm(name):
            o, w = _COL[name]
            return pr_m[:, o:o + w], pr_t[:, o:o + w], w
        Lp = n_meta + Lmain
        fk = colm('fox_k'); fv = colm('fox_v'); dv = colm('dsa_v'); xbc = colm('ssd_xbc')

        def gla_state(S, B):
            S4 = S.reshape(B, HEADS, GLA_DK, HEADS, 64)
            return jnp.stack([S4[:, hh, :, hh, :] for hh in range(HEADS)], axis=1)

        def ssd_state(H, B):
            H4 = H.reshape(B, 2, 64, HEADS, 64)
            return jnp.stack([jnp.swapaxes(H4[:, hh // 2, :, hh, :], 1, 2) for hh in range(HEADS)], axis=1)

        xbc_p = seq_p(xbc[0], xbc[1], 512)
        xbc_s = seq_s(xbc[1], 512)
        layer_p = (seq_p(fk[0], fk[1], 256).reshape(Bp, Lp, HEADS, HD),
                   seq_p(fv[0], fv[1], 256).reshape(Bp, Lp, HEADS, HD),
                   seq_p(lf_m, lf_t, LANES)[:, :, :HEADS],
                   seq_p(dk_m, dk_t, 256).reshape(Bp, Lp, HEADS, HD),
                   seq_p(dv[0], dv[1], 256).reshape(Bp, Lp, HEADS, HD),
                   seq_p(ik_m, ik_t, LANES)[:, :, :HD],
                   gla_state(gla_S_p, Bp), ssd_state(ssd_H_p, Bp),
                   xbc_p[:, Lp - (SSD_CONV - 1):])
        conv_s = jnp.concatenate([state_ssd_conv[l].astype(F32), xbc_s], axis=1)[:, Ls:]
        layer_s = (seq_s(fk[1], 256).reshape(Bs, Ls, HEADS, HD),
                   seq_s(fv[1], 256).reshape(Bs, Ls, HEADS, HD),
                   seq_s(lf_t, LANES)[:, :, :HEADS],
                   seq_s(dk_t, 256).reshape(Bs, Ls, HEADS, HD),
                   seq_s(dv[1], 256).reshape(Bs, Ls, HEADS, HD),
                   seq_s(ik_t, LANES)[:, :, :HD],
                   gla_state(gla_S_s, Bs), ssd_state(ssd_H_s, Bs),
                   conv_s)
        for k_ in range(9):
            outs_p[k_].append(layer_p[k_])
            outs_s[k_].append(layer_s[k_])

    y_prompt = y_m.reshape(Bp, Lmain, D)
    y_sample = y_t[Bp * n_meta:].reshape(Bs, Ls, D)
    return (y_prompt, y_sample) + tuple(jnp.stack(c) for c in outs_p) + tuple(jnp.stack(c) for c in outs_s)
```
